```python
import math
import jax, jax.numpy as jnp
from jax import lax
import numpy as np

D_MODEL = 1024
BATCH = 4
SEQ = 4096
DEPTH = 4

CHUNK = 64
Q_BLOCK = 128
N_A_LAYERS = DEPTH // 2
N_B_LAYERS = DEPTH - N_A_LAYERS
D_FF = 2816
D_RNN = D_MODEL
N_LRU_BLOCKS = 8
LRU_BLOCK = D_RNN // N_LRU_BLOCKS
CONV_WIDTH = 4
LRU_C = 8.0
N_HEADS = 8
HEAD_DIM = D_MODEL // (2 * N_HEADS)
ROPE_THETA = 10000.0
EPS = 1e-6
SUBLN_EPS = 1e-5

kernel_name = "hybrid_rglru_diffattn_yoco_macaron"


def rmsnorm(x, g, eps=EPS):
    xf = x.astype(jnp.float32)
    y = xf * lax.rsqrt(jnp.mean(xf * xf, axis=-1, keepdims=True) + eps)
    return (y * g.astype(jnp.float32)).astype(x.dtype)


def swiglu(h, w_gate, w_up, w_down):
    return (jax.nn.silu(h @ w_gate) * (h @ w_up)) @ w_down


def causal_depthwise_conv(x, w, b):
    S = x.shape[1]
    xp = jnp.pad(x, ((0, 0), (CONV_WIDTH - 1, 0), (0, 0)))
    out = b
    for k in range(CONV_WIDTH):
        out = out + xp[:, k:k + S, :] * w[k]
    return out


def _lru_combine(c1, c2):
    a1, b1 = c1
    a2, b2 = c2
    return a1 * a2, a2 * b1 + b2


def rg_lru(x, w_a, b_a, w_x, b_x, lam):
    B, S, _ = x.shape
    xb = x.reshape(B, S, N_LRU_BLOCKS, LRU_BLOCK)
    gate_a = jnp.einsum('bsnh,nhk->bsnk', xb, w_a).reshape(B, S, D_RNN) + b_a
    gate_x = jnp.einsum('bsnh,nhk->bsnk', xb, w_x).reshape(B, S, D_RNN) + b_x
    r = jax.nn.sigmoid(gate_a.astype(jnp.float32))
    i = jax.nn.sigmoid(gate_x.astype(jnp.float32))
    log_a = -LRU_C * r * jax.nn.softplus(-lam.astype(jnp.float32))
    a = jnp.exp(log_a)
    mult = jnp.sqrt(-jnp.expm1(2.0 * log_a))
    bx = mult * (i * x.astype(jnp.float32))
    _, h = lax.associative_scan(_lru_combine, (a, bx), axis=1)
    return h.astype(x.dtype)


def recurrent_block(h, w_in, conv_w, conv_b, w_a, b_a, w_x, b_x, lam, w_out):
    proj = h @ w_in
    gate, rec = proj[..., :D_RNN], proj[..., D_RNN:]
    rec = causal_depthwise_conv(rec, conv_w, conv_b)
    rec = rg_lru(rec, w_a, b_a, w_x, b_x, lam)
    return (jax.nn.gelu(gate) * rec) @ w_out


def rope_tables(S):
    pos = jnp.arange(S, dtype=jnp.float32)
    inv_freq = ROPE_THETA ** (-jnp.arange(0, HEAD_DIM, 2, dtype=jnp.float32) / HEAD_DIM)
    ang = pos[:, None] * inv_freq[None, :]
    ang = jnp.concatenate([ang, ang], axis=-1)
    return jnp.cos(ang), jnp.sin(ang)


def apply_rope(t, cos, sin):
    tf = t.astype(jnp.float32)
    half = HEAD_DIM // 2
    rot = jnp.concatenate([-tf[..., half:], tf[..., :half]], axis=-1)
    c = cos[None, :, None, None, :]
    s = sin[None, :, None, None, :]
    return (tf * c + rot * s).astype(t.dtype)


def shared_kv(x, kv_norm, w_k, w_v, k_norm, cos, sin):
    B, S, _ = x.shape
    h = rmsnorm(x, kv_norm)
    k = (h @ w_k).reshape(B, S, N_HEADS, 2, HEAD_DIM)
    k = apply_rope(rmsnorm(k, k_norm), cos, sin)
    k = k.transpose(0, 2, 3, 1, 4)
    v = (h @ w_v).reshape(B, S, N_HEADS, 2 * HEAD_DIM).transpose(0, 2, 1, 3)
    return k, v


def diff_attention(h, k, v, w_q, q_norm, lq1, lq2, lk1, lk2, sub_norm, w_o, cos, sin, lambda_init):
    B, S, _ = h.shape
    nb = S // Q_BLOCK
    q = (h @ w_q).reshape(B, S, N_HEADS, 2, HEAD_DIM)
    q = apply_rope(rmsnorm(q, q_norm), cos, sin) * (HEAD_DIM ** -0.5)
    q_blocks = q.reshape(B, nb, Q_BLOCK, N_HEADS, 2, HEAD_DIM).transpose(1, 0, 3, 4, 2, 5)
    lam = (jnp.exp(jnp.sum((lq1 * lk1).astype(jnp.float32)))
           - jnp.exp(jnp.sum((lq2 * lk2).astype(jnp.float32))) + lambda_init)
    key_chunk = jnp.arange(S) // CHUNK
    vf = v.astype(jnp.float32)

    def one_block(args):
        qb, bi = args
        scores = jnp.einsum('bhcqd,bhckd->bhcqk', qb, k, preferred_element_type=jnp.float32)
        q_chunk = (bi * Q_BLOCK + jnp.arange(Q_BLOCK)) // CHUNK
        mask = key_chunk[None, :] <= q_chunk[:, None]
        p = jax.nn.softmax(jnp.where(mask, scores, -jnp.inf), axis=-1)
        attn = p[:, :, 0] - lam * p[:, :, 1]
        return jnp.einsum('bhqk,bhkd->bhqd', attn, vf)

    o = lax.map(one_block, (q_blocks, jnp.arange(nb)))
    o = o.transpose(1, 0, 3, 2, 4).reshape(B, S, N_HEADS, 2 * HEAD_DIM)
    o = rmsnorm(o, sub_norm, SUBLN_EPS) * (1.0 - lambda_init)
    return o.reshape(B, S, D_MODEL).astype(h.dtype) @ w_o


def setup_inputs(seed: int = 0) -> dict:
    key = jax.random.key(seed)
    ks = iter(jax.random.split(key, 64))
    f32 = jnp.float32

    def dense(shape, fan_in):
        return jax.random.normal(next(ks), shape, f32) * (fan_in ** -0.5)

    def gain(shape):
        return 1.0 + 0.02 * jax.random.normal(next(ks), shape, f32)

    def bias(shape):
        return 0.01 * jax.random.normal(next(ks), shape, f32)

    nA, nB = N_A_LAYERS, N_B_LAYERS
    u = jax.random.uniform(next(ks), (nA, D_RNN), f32, 0.9, 0.999)
    a0 = u ** (1.0 / LRU_C)
    rec_lambda = jnp.log(a0) - jnp.log1p(-a0)
    return {
        'x': jax.random.normal(next(ks), (BATCH, SEQ, D_MODEL), f32),
        'ffn1_norm': gain((DEPTH, D_MODEL)),
        'ffn1_w_gate': dense((DEPTH, D_MODEL, D_FF), D_MODEL),
        'ffn1_w_up': dense((DEPTH, D_MODEL, D_FF), D_MODEL),
        'ffn1_w_down': dense((DEPTH, D_FF, D_MODEL), D_FF),
        'ffn2_norm': gain((DEPTH, D_MODEL)),
        'ffn2_w_gate': dense((DEPTH, D_MODEL, D_FF), D_MODEL),
        'ffn2_w_up': dense((DEPTH, D_MODEL, D_FF), D_MODEL),
        'ffn2_w_down': dense((DEPTH, D_FF, D_MODEL), D_FF),
        'mix_norm': gain((DEPTH, D_MODEL)),
        'rec_w_in': dense((nA, D_MODEL, 2 * D_RNN), D_MODEL),
        'rec_conv_w': dense((nA, CONV_WIDTH, D_RNN), CONV_WIDTH),
        'rec_conv_b': bias((nA, D_RNN)),
        'rec_w_a': dense((nA, N_LRU_BLOCKS, LRU_BLOCK, LRU_BLOCK), LRU_BLOCK),
        'rec_b_a': bias((nA, D_RNN)),
        'rec_w_x': dense((nA, N_LRU_BLOCKS, LRU_BLOCK, LRU_BLOCK), LRU_BLOCK),
        'rec_b_x': bias((nA, D_RNN)),
        'rec_lambda': rec_lambda,
        'rec_w_out': dense((nA, D_RNN, D_MODEL), D_RNN),
        'kv_norm': gain((D_MODEL,)),
        'w_k': dense((D_MODEL, 2 * N_HEADS * HEAD_DIM), D_MODEL),
        'w_v': dense((D_MODEL, 2 * N_HEADS * HEAD_DIM), D_MODEL),
        'k_norm': gain((HEAD_DIM,)),
        'lambda_k1': 0.1 * jax.random.normal(next(ks), (HEAD_DIM,), f32),
        'lambda_k2': 0.1 * jax.random.normal(next(ks), (HEAD_DIM,), f32),
        'attn_w_q': dense((nB, D_MODEL, 2 * N_HEADS * HEAD_DIM), D_MODEL),
        'q_norm': gain((nB, HEAD_DIM)),
        'lambda_q1': 0.1 * jax.random.normal(next(ks), (nB, HEAD_DIM), f32),
        'lambda_q2': 0.1 * jax.random.normal(next(ks), (nB, HEAD_DIM), f32),
        'sub_norm': gain((nB, 2 * HEAD_DIM)),
        'attn_w_o': dense((nB, 2 * N_HEADS * HEAD_DIM, D_MODEL), 2 * N_HEADS * HEAD_DIM),
    }


def reference(x, ffn1_norm, ffn1_w_gate, ffn1_w_up, ffn1_w_down,
              ffn2_norm, ffn2_w_gate, ffn2_w_up, ffn2_w_down, mix_norm,
              rec_w_in, rec_conv_w, rec_conv_b, rec_w_a, rec_b_a, rec_w_x, rec_b_x,
              rec_lambda, rec_w_out, kv_norm, w_k, w_v, k_norm, lambda_k1, lambda_k2,
              attn_w_q, q_norm, lambda_q1, lambda_q2, sub_norm, attn_w_o):
    S = x.shape[1]
    cos, sin = rope_tables(S)
    k_shared, v_shared = None, None
    for layer in range(DEPTH):
        if layer == N_A_LAYERS:
            k_shared, v_shared = shared_kv(x, kv_norm, w_k, w_v, k_norm, cos, sin)
        x = x + 0.5 * swiglu(rmsnorm(x, ffn1_norm[layer]), ffn1_w_gate[layer],
                             ffn1_w_up[layer], ffn1_w_down[layer])
        h = rmsnorm(x, mix_norm[layer])
        if layer < N_A_LAYERS:
            a = layer
            x = x + recurrent_block(h, rec_w_in[a], rec_conv_w[a], rec_conv_b[a], rec_w_a[a],
                                    rec_b_a[a], rec_w_x[a], rec_b_x[a], rec_lambda[a], rec_w_out[a])
        else:
            j = layer - N_A_LAYERS
            lambda_init = 0.8 - 0.6 * math.exp(-0.3 * layer)
            x = x + diff_attention(h, k_shared, v_shared, attn_w_q[j], q_norm[j], lambda_q1[j],
                                   lambda_q2[j], lambda_k1, lambda_k2, sub_norm[j], attn_w_o[j],
                                   cos, sin, lambda_init)
        x = x + 0.5 * swiglu(rmsnorm(x, ffn2_norm[layer]), ffn2_w_gate[layer],
                             ffn2_w_up[layer], ffn2_w_down[layer])
    return x
```

```python
import functools
import math

import jax
import jax.numpy as jnp
from jax import lax
from jax.experimental import pallas as pl
from jax.experimental.pallas import tpu as pltpu

F32 = jnp.float32
BF16 = jnp.bfloat16

D_MODEL = 1024
D_FF = 2816
DEPTH = 4
N_A_LAYERS = DEPTH // 2
CHUNK = 64
D_RNN = D_MODEL
N_LRU_BLOCKS = 8
LRU_BLOCK = D_RNN // N_LRU_BLOCKS
CONV_WIDTH = 4
LRU_C = 8.0
N_HEADS = 8
HEAD_DIM = 64
ROPE_THETA = 10000.0
EPS = 1e-6
SUBLN_EPS = 1e-5

LANES = 128
SUBLANES = 8
VMEM_LIMIT = 56 * 1024 * 1024

FFN_ROWS = 512
FFN_COLS = 256
SEQ_ROWS = 512
ATT_Q = 512
ATT_K = 512


def _rms(xf, g, eps):
    ms = jnp.mean(xf * xf, axis=-1, keepdims=True)
    return xf * lax.rsqrt(ms + eps) * g


def _params(sem):
    return pltpu.CompilerParams(dimension_semantics=sem, vmem_limit_bytes=VMEM_LIMIT)


def _const_spec(shape):
    nd = len(shape)
    return pl.BlockSpec(shape, lambda *_: (0,) * nd, pipeline_mode=pl.Buffered(1))


def _ffn_kernel(x_ref, g_ref, wg_ref, wu_ref, wd_ref, o_ref, act_ref):
    x = x_ref[...]
    h = _rms(x, g_ref[...], EPS).astype(BF16)
    for c in range(D_FF // FFN_COLS):
        sl = slice(c * FFN_COLS, (c + 1) * FFN_COLS)
        gate = jnp.dot(h, wg_ref[:, sl], preferred_element_type=F32)
        up = jnp.dot(h, wu_ref[:, sl], preferred_element_type=F32)
        act_ref[:, sl] = (gate * jax.nn.sigmoid(gate) * up).astype(BF16)
    y = jnp.dot(act_ref[...], wd_ref[...], preferred_element_type=F32)
    o_ref[...] = x + 0.5 * y


def _ffn(x2d, g, wg, wu, wd):
    m = x2d.shape[0]
    row_spec = pl.BlockSpec((FFN_ROWS, D_MODEL), lambda i: (i, 0))
    return pl.pallas_call(
        _ffn_kernel,
        out_shape=jax.ShapeDtypeStruct((m, D_MODEL), F32),
        grid=(m // FFN_ROWS,),
        in_specs=[row_spec, _const_spec((1, D_MODEL)), _const_spec((D_MODEL, D_FF)),
                  _const_spec((D_MODEL, D_FF)), _const_spec((D_FF, D_MODEL))],
        out_specs=row_spec,
        scratch_shapes=[pltpu.VMEM((FFN_ROWS, D_FF), BF16)],
        compiler_params=_params(("arbitrary",)),
        name="ffn",
    )(x2d, g, wg, wu, wd)


def _rec_kernel(x_ref, g_ref, win_ref, cw_ref, cb_ref, wax_ref, ba_ref, bx_ref, lam_ref,
                wout_ref, o_ref, ext_ref, a_ref, b_ref, carry_ref):
    ts = SEQ_ROWS

    @pl.when(pl.program_id(1) == 0)
    def _():
        ext_ref[0:SUBLANES, :] = jnp.zeros((SUBLANES, D_RNN), F32)
        carry_ref[...] = jnp.zeros((SUBLANES, D_RNN), F32)

    x = x_ref[0]
    h = _rms(x, g_ref[...], EPS).astype(BF16)
    proj = jnp.dot(h, win_ref[...], preferred_element_type=F32)
    gate = proj[:, :D_RNN]

    ext_ref[SUBLANES:SUBLANES + ts, :] = proj[:, D_RNN:]
    conv = cb_ref[...]
    for k in range(CONV_WIDTH):
        start = SUBLANES - (CONV_WIDTH - 1) + k
        conv = conv + ext_ref[start:start + ts, :] * cw_ref[k:k + 1, :]
    ext_ref[0:SUBLANES, :] = ext_ref[ts:ts + SUBLANES, :]

    lam = lam_ref[...]
    neg = -lam
    softplus = jnp.maximum(neg, 0.0) + jnp.log1p(jnp.exp(-jnp.abs(neg)))
    conv_bf = conv.astype(BF16)
    for n in range(N_LRU_BLOCKS):
        sl = slice(n * LRU_BLOCK, (n + 1) * LRU_BLOCK)
        gax = jnp.dot(conv_bf[:, sl], wax_ref[n], preferred_element_type=F32)
        r = jax.nn.sigmoid(gax[:, :LRU_BLOCK] + ba_ref[:, sl])
        i = jax.nn.sigmoid(gax[:, LRU_BLOCK:] + bx_ref[:, sl])
        log_a = -LRU_C * r * softplus[:, sl]
        a = jnp.exp(log_a)
        a_ref[:, sl] = a
        one_minus_a2 = -jnp.tanh(log_a) * (a * a + 1.0)
        b_ref[:, sl] = jnp.sqrt(one_minus_a2) * (i * conv[:, sl])

    row = lax.broadcasted_iota(jnp.int32, (SUBLANES, D_RNN), 0)

    def group(gi, carry):
        off = pl.multiple_of(gi * SUBLANES, SUBLANES)
        a = a_ref[pl.ds(off, SUBLANES), :]
        b = b_ref[pl.ds(off, SUBLANES), :]
        for sh in (1, 2, 4):
            keep = row >= sh
            a_prev = jnp.where(keep, pltpu.roll(a, sh, 0), 1.0)
            b_prev = jnp.where(keep, pltpu.roll(b, sh, 0), 0.0)
            b = a * b_prev + b
            a = a * a_prev
        hh = a * carry + b
        b_ref[pl.ds(off, SUBLANES), :] = hh
        return jnp.broadcast_to(hh[SUBLANES - 1:SUBLANES, :], (SUBLANES, D_RNN))

    carry_ref[...] = lax.fori_loop(0, ts // SUBLANES, group, carry_ref[...], unroll=2)

    y = (jax.nn.gelu(gate, approximate=True) * b_ref[...]).astype(BF16)
    o_ref[0] = x + jnp.dot(y, wout_ref[...], preferred_element_type=F32)


def _rec(x3d, g, win, cw, cb, wax, ba, bx, lam, wout):
    b, s, _ = x3d.shape
    seq_spec = pl.BlockSpec((1, SEQ_ROWS, D_MODEL), lambda bi, si: (bi, si, 0))
    vec = _const_spec((1, D_RNN))
    return pl.pallas_call(
        _rec_kernel,
        out_shape=jax.ShapeDtypeStruct((b, s, D_MODEL), F32),
        grid=(b, s // SEQ_ROWS),
        in_specs=[seq_spec, _const_spec((1, D_MODEL)), _const_spec((D_MODEL, 2 * D_RNN)),
                  _const_spec((CONV_WIDTH, D_RNN)), vec,
                  _const_spec((N_LRU_BLOCKS, LRU_BLOCK, 2 * LRU_BLOCK)), vec, vec, vec,
                  _const_spec((D_RNN, D_MODEL))],
        out_specs=seq_spec,
        scratch_shapes=[pltpu.VMEM((SEQ_ROWS + SUBLANES, D_RNN), F32),
                        pltpu.VMEM((SEQ_ROWS, D_RNN), F32),
                        pltpu.VMEM((SEQ_ROWS, D_RNN), F32),
                        pltpu.VMEM((SUBLANES, D_RNN), F32)],
        compiler_params=_params(("arbitrary", "arbitrary")),
        name="rec_block",
    )(x3d, g, win, cw, cb, wax, ba, bx, lam, wout)


def _head_norm_rope(t, hn, cos, sin_signed, first_half, ones_blk):
    sq = t * t
    hi = sq.astype(BF16)
    lo = (sq - hi.astype(F32)).astype(BF16)
    ssum = (jnp.dot(hi, ones_blk, preferred_element_type=F32)
            + jnp.dot(lo, ones_blk, preferred_element_type=F32))
    tn = t * lax.rsqrt(ssum * (1.0 / HEAD_DIM) + EPS) * hn
    partner = jnp.where(first_half, pltpu.roll(tn, LANES - HEAD_DIM // 2, 1),
                        pltpu.roll(tn, HEAD_DIM // 2, 1))
    return tn * cos + partner * sin_signed


def _rope_consts(rows):
    lane = lax.broadcasted_iota(jnp.int32, (rows, LANES), 1)
    first_half = (lane % HEAD_DIM) < (HEAD_DIM // 2)
    r = lax.broadcasted_iota(jnp.int32, (LANES, LANES), 0) // HEAD_DIM
    c = lax.broadcasted_iota(jnp.int32, (LANES, LANES), 1) // HEAD_DIM
    ones_blk = jnp.where(r == c, 1.0, 0.0).astype(BF16)
    return first_half, ones_blk


def _kv_kernel(x_ref, g_ref, wk_ref, wv_ref, hn_ref, cos_ref, sin_ref, kt_ref, v_ref):
    x = x_ref[0]
    h = _rms(x, g_ref[...], EPS).astype(BF16)
    k = jnp.dot(h, wk_ref[...], preferred_element_type=F32)
    v = jnp.dot(h, wv_ref[...], preferred_element_type=F32)
    first_half, ones_blk = _rope_consts(SEQ_ROWS)
    cos = cos_ref[...]
    sin_signed = sin_ref[...]
    hn = hn_ref[...]
    for c in range(N_HEADS):
        sl = slice(c * LANES, (c + 1) * LANES)
        kr = _head_norm_rope(k[:, sl], hn, cos, sin_signed, first_half, ones_blk)
        kt_ref[0, c, 0] = kr.T.astype(BF16)
        v_ref[0, c] = v[:, sl].astype(BF16)


def _kv(x3d, g, wk, wv, hn, cos, sin_signed):
    b, s, _ = x3d.shape
    nt = s // SEQ_ROWS
    seq_spec = pl.BlockSpec((1, SEQ_ROWS, D_MODEL), lambda bi, si: (bi, si, 0))
    tab = pl.BlockSpec((SEQ_ROWS, LANES), lambda bi, si: (si, 0))
    return pl.pallas_call(
        _kv_kernel,
        out_shape=(jax.ShapeDtypeStruct((b, N_HEADS, nt, LANES, SEQ_ROWS), BF16),
                   jax.ShapeDtypeStruct((b, N_HEADS, s, LANES), BF16)),
        grid=(b, nt),
        in_specs=[seq_spec, _const_spec((1, D_MODEL)), _const_spec((D_MODEL, D_MODEL)),
                  _const_spec((D_MODEL, D_MODEL)), _const_spec((1, LANES)), tab, tab],
        out_specs=(pl.BlockSpec((1, N_HEADS, 1, LANES, SEQ_ROWS), lambda bi, si: (bi, 0, si, 0, 0)),
                   pl.BlockSpec((1, N_HEADS, SEQ_ROWS, LANES), lambda bi, si: (bi, 0, si, 0))),
        compiler_params=_params(("arbitrary", "arbitrary")),
        name="kv_proj",
    )(x3d, g, wk, wv, hn, cos, sin_signed)


def _q_kernel(x_ref, g_ref, wq_ref, hn_ref, cos_ref, sin_ref, q_ref):
    x = x_ref[0]
    h = _rms(x, g_ref[...], EPS).astype(BF16)
    q = jnp.dot(h, wq_ref[...], preferred_element_type=F32)
    first_half, ones_blk = _rope_consts(SEQ_ROWS)
    cos = cos_ref[...]
    sin_signed = sin_ref[...]
    hn = hn_ref[...]
    for c in range(N_HEADS):
        sl = slice(c * LANES, (c + 1) * LANES)
        qr = _head_norm_rope(q[:, sl], hn, cos, sin_signed, first_half, ones_blk)
        q_ref[0, c] = (qr * (HEAD_DIM ** -0.5)).astype(BF16)


def _qproj(x3d, g, wq, hn, cos, sin_signed):
    b, s, _ = x3d.shape
    seq_spec = pl.BlockSpec((1, SEQ_ROWS, D_MODEL), lambda bi, si: (bi, si, 0))
    tab = pl.BlockSpec((SEQ_ROWS, LANES), lambda bi, si: (si, 0))
    return pl.pallas_call(
        _q_kernel,
        out_shape=jax.ShapeDtypeStruct((b, N_HEADS, s, LANES), BF16),
        grid=(b, s // SEQ_ROWS),
        in_specs=[seq_spec, _const_spec((1, D_MODEL)), _const_spec((D_MODEL, D_MODEL)),
                  _const_spec((1, LANES)), tab, tab],
        out_specs=pl.BlockSpec((1, N_HEADS, SEQ_ROWS, LANES), lambda bi, si: (bi, 0, si, 0)),
        compiler_params=_params(("arbitrary", "arbitrary")),
        name="q_proj",
    )(x3d, g, wq, hn, cos, sin_signed)


def _attn_kernel(q_ref, kt_ref, v_ref, lq1_ref, lk1_ref, lq2_ref, lk2_ref, sn_ref, o_ref, *,
                 lambda_init):
    qi = pl.program_id(2)
    q = q_ref[0, 0]
    lane = lax.broadcasted_iota(jnp.int32, (ATT_Q, LANES), 1)
    zero = jnp.zeros_like(q)
    qs = (jnp.where(lane < HEAD_DIM, q, zero), jnp.where(lane >= HEAD_DIM, q, zero))

    q_chunk = lax.broadcasted_iota(jnp.int32, (ATT_Q, ATT_K), 0) // CHUNK
    k_chunk = lax.broadcasted_iota(jnp.int32, (ATT_Q, ATT_K), 1) // CHUNK
    diag_mask = k_chunk <= q_chunk

    kt = kt_ref[0, 0, qi]
    vv = v_ref[0, 0, pl.ds(pl.multiple_of(qi * ATT_K, ATT_K), ATT_K), :]
    state = []
    for qc in qs:
        s = jnp.dot(qc, kt, preferred_element_type=F32)
        s = jnp.where(diag_mask, s, -jnp.inf)
        m = jnp.max(s, axis=-1, keepdims=True)
        p = jnp.exp(s - m)
        l = jnp.sum(p, axis=-1, keepdims=True)
        acc = jnp.dot(p.astype(BF16), vv, preferred_element_type=F32)
        state += [m, l, acc]

    def kv_tile(j, st):
        kt = kt_ref[0, 0, j]
        vv = v_ref[0, 0, pl.ds(pl.multiple_of(j * ATT_K, ATT_K), ATT_K), :]
        out = []
        for c, qc in enumerate(qs):
            m_old, l_old, acc_old = st[3 * c:3 * c + 3]
            s = jnp.dot(qc, kt, preferred_element_type=F32)
            m = jnp.maximum(m_old, jnp.max(s, axis=-1, keepdims=True))
            alpha = jnp.exp(m_old - m)
            p = jnp.exp(s - m)
            l = alpha * l_old + jnp.sum(p, axis=-1, keepdims=True)
            acc = alpha * acc_old + jnp.dot(p.astype(BF16), vv, preferred_element_type=F32)
            out += [m, l, acc]
        return tuple(out)

    _, l1, acc1, _, l2, acc2 = lax.fori_loop(0, qi, kv_tile, tuple(state))

    lam = (jnp.exp(jnp.sum(lq1_ref[...] * lk1_ref[...], axis=-1, keepdims=True))
           - jnp.exp(jnp.sum(lq2_ref[...] * lk2_ref[...], axis=-1, keepdims=True))
           + lambda_init)
    o = acc1 / l1 - lam * (acc2 / l2)
    o = _rms(o, sn_ref[...], SUBLN_EPS) * (1.0 - lambda_init)
    o_ref[0] = o.astype(BF16)


def _attn(q, kt, v, lq1, lk1, lq2, lk2, sn, lambda_init):
    b, nh, s, _ = q.shape
    nt = s // ATT_K
    vec = _const_spec((1, HEAD_DIM))
    return pl.pallas_call(
        functools.partial(_attn_kernel, lambda_init=lambda_init),
        out_shape=jax.ShapeDtypeStruct((b, s, D_MODEL), BF16),
        grid=(b, nh, s // ATT_Q),
        in_specs=[pl.BlockSpec((1, 1, ATT_Q, LANES), lambda bi, hi, qi: (bi, hi, qi, 0)),
                  pl.BlockSpec((1, 1, nt, LANES, ATT_K), lambda bi, hi, qi: (bi, hi, 0, 0, 0)),
                  pl.BlockSpec((1, 1, s, LANES), lambda bi, hi, qi: (bi, hi, 0, 0)),
                  vec, vec, vec, vec, _const_spec((1, LANES))],
        out_specs=pl.BlockSpec((1, ATT_Q, LANES), lambda bi, hi, qi: (bi, qi, hi)),
        compiler_params=_params(("arbitrary", "arbitrary", "arbitrary")),
        name="diff_attn",
    )(q, kt, v, lq1, lk1, lq2, lk2, sn)


def _oproj_kernel(x_ref, o_ref, w_ref, out_ref):
    out_ref[...] = x_ref[...] + jnp.dot(o_ref[...], w_ref[...], preferred_element_type=F32)


def _oproj(x2d, o2d, w):
    m = x2d.shape[0]
    row_spec = pl.BlockSpec((FFN_ROWS, D_MODEL), lambda i: (i, 0))
    return pl.pallas_call(
        _oproj_kernel,
        out_shape=jax.ShapeDtypeStruct((m, D_MODEL), F32),
        grid=(m // FFN_ROWS,),
        in_specs=[row_spec, row_spec, _const_spec((D_MODEL, D_MODEL))],
        out_specs=row_spec,
        compiler_params=_params(("arbitrary",)),
        name="attn_out_proj",
    )(x2d, o2d, w)


def _rope_tables(s):
    pos = jnp.arange(s, dtype=F32)
    inv_freq = ROPE_THETA ** (-jnp.arange(0, HEAD_DIM, 2, dtype=F32) / HEAD_DIM)
    ang = pos[:, None] * inv_freq[None, :]
    ang = jnp.concatenate([ang, ang], axis=-1)
    cos, sin = jnp.cos(ang), jnp.sin(ang)
    half = HEAD_DIM // 2
    sin_signed = jnp.concatenate([-sin[:, :half], sin[:, half:]], axis=-1)
    reps = LANES // HEAD_DIM
    return jnp.tile(cos, (1, reps)), jnp.tile(sin_signed, (1, reps))


def kernel(x, ffn1_norm, ffn1_w_gate, ffn1_w_up, ffn1_w_down, ffn2_norm, ffn2_w_gate, ffn2_w_up, ffn2_w_down, mix_norm, rec_w_in, rec_conv_w, rec_conv_b, rec_w_a, rec_b_a, rec_w_x, rec_b_x, rec_lambda, rec_w_out, kv_norm, w_k, w_v, k_norm, lambda_k1, lambda_k2, attn_w_q, q_norm, lambda_q1, lambda_q2, sub_norm, attn_w_o):
    b, s, d = x.shape
    m = b * s
    reps = LANES // HEAD_DIM
    cos, sin_signed = _rope_tables(s)

    f1 = (ffn1_w_gate.astype(BF16), ffn1_w_up.astype(BF16), ffn1_w_down.astype(BF16))
    f2 = (ffn2_w_gate.astype(BF16), ffn2_w_up.astype(BF16), ffn2_w_down.astype(BF16))
    w_in = rec_w_in.astype(BF16)
    w_ax = jnp.concatenate([rec_w_a, rec_w_x], axis=-1).astype(BF16)
    w_out = rec_w_out.astype(BF16)
    wk, wv = w_k.astype(BF16), w_v.astype(BF16)
    wq, wo = attn_w_q.astype(BF16), attn_w_o.astype(BF16)

    def row(vec):
        return vec.reshape(1, -1)

    kt = v = None
    for layer in range(DEPTH):
        if layer == N_A_LAYERS:
            kt, v = _kv(x, row(kv_norm), wk, wv, row(jnp.tile(k_norm, reps)), cos, sin_signed)
        x = _ffn(x.reshape(m, d), row(ffn1_norm[layer]), f1[0][layer], f1[1][layer],
                 f1[2][layer]).reshape(b, s, d)
        if layer < N_A_LAYERS:
            a = layer
            x = _rec(x, row(mix_norm[layer]), w_in[a], rec_conv_w[a], row(rec_conv_b[a]), w_ax[a],
                     row(rec_b_a[a]), row(rec_b_x[a]), row(rec_lambda[a]), w_out[a])
        else:
            j = layer - N_A_LAYERS
            lambda_init = 0.8 - 0.6 * math.exp(-0.3 * layer)
            q = _qproj(x, row(mix_norm[layer]), wq[j], row(jnp.tile(q_norm[j], reps)), cos,
                       sin_signed)
            o = _attn(q, kt, v, row(lambda_q1[j]), row(lambda_k1), row(lambda_q2[j]),
                      row(lambda_k2), row(sub_norm[j]), lambda_init)
            x = _oproj(x.reshape(m, d), o.reshape(m, d), wo[j]).reshape(b, s, d)
        x = _ffn(x.reshape(m, d), row(ffn2_norm[layer]), f2[0][layer], f2[1][layer],
                 f2[2][layer]).reshape(b, s, d)
    return x
```

```python
import functools
import math

import jax
import jax.numpy as jnp
from jax import lax
from jax.experimental import pallas as pl
from jax.experimental.pallas import tpu as pltpu

F32 = jnp.float32
BF16 = jnp.bfloat16

D_MODEL = 1024
D_FF = 2816
DEPTH = 4
N_A_LAYERS = DEPTH // 2
CHUNK = 64
D_RNN = D_MODEL
N_LRU_BLOCKS = 8
LRU_BLOCK = D_RNN // N_LRU_BLOCKS
CONV_WIDTH = 4
LRU_C = 8.0
N_HEADS = 8
HEAD_DIM = 64
ROPE_THETA = 10000.0
EPS = 1e-6
SUBLN_EPS = 1e-5

LANES = 128
SUBLANES = 8
VMEM_LIMIT = 56 * 1024 * 1024

FFN_ROWS = 512
FFN_COLS = 256
SEQ_ROWS = 512
ATT_Q = 512
ATT_K = 512
ATT_HEADS = 2
ATT_ROWS = 32
LOG2E = 1.4426950408889634


def _rms(xf, g, eps):
    ms = jnp.mean(xf * xf, axis=-1, keepdims=True)
    return xf * lax.rsqrt(ms + eps) * g


def _params(sem):
    return pltpu.CompilerParams(dimension_semantics=sem, vmem_limit_bytes=VMEM_LIMIT)


def _const_spec(shape):
    nd = len(shape)
    return pl.BlockSpec(shape, lambda *_: (0,) * nd, pipeline_mode=pl.Buffered(1))


def _ffn_kernel(x_ref, g_ref, wg_ref, wu_ref, wd_ref, o_ref, act_ref):
    x = x_ref[...]
    h = _rms(x, g_ref[...], EPS).astype(BF16)
    for c in range(D_FF // FFN_COLS):
        sl = slice(c * FFN_COLS, (c + 1) * FFN_COLS)
        gate = jnp.dot(h, wg_ref[:, sl], preferred_element_type=F32)
        up = jnp.dot(h, wu_ref[:, sl], preferred_element_type=F32)
        act_ref[:, sl] = (gate * jax.nn.sigmoid(gate) * up).astype(BF16)
    y = jnp.dot(act_ref[...], wd_ref[...], preferred_element_type=F32)
    o_ref[...] = x + 0.5 * y


def _ffn(x2d, g, wg, wu, wd):
    m = x2d.shape[0]
    row_spec = pl.BlockSpec((FFN_ROWS, D_MODEL), lambda i: (i, 0))
    return pl.pallas_call(
        _ffn_kernel,
        out_shape=jax.ShapeDtypeStruct((m, D_MODEL), F32),
        grid=(m // FFN_ROWS,),
        in_specs=[row_spec, _const_spec((1, D_MODEL)), _const_spec((D_MODEL, D_FF)),
                  _const_spec((D_MODEL, D_FF)), _const_spec((D_FF, D_MODEL))],
        out_specs=row_spec,
        scratch_shapes=[pltpu.VMEM((FFN_ROWS, D_FF), BF16)],
        compiler_params=_params(("arbitrary",)),
        name="ffn",
    )(x2d, g, wg, wu, wd)


def _rec_kernel(x_ref, g_ref, win_ref, cw_ref, cb_ref, wax_ref, ba_ref, bx_ref, lam_ref,
                wout_ref, o_ref, ext_ref, a_ref, b_ref, carry_ref):
    ts = SEQ_ROWS

    @pl.when(pl.program_id(1) == 0)
    def _():
        ext_ref[0:SUBLANES, :] = jnp.zeros((SUBLANES, D_RNN), F32)
        carry_ref[...] = jnp.zeros((SUBLANES, D_RNN), F32)

    x = x_ref[0]
    h = _rms(x, g_ref[...], EPS).astype(BF16)
    proj = jnp.dot(h, win_ref[...], preferred_element_type=F32)
    gate = proj[:, :D_RNN]

    ext_ref[SUBLANES:SUBLANES + ts, :] = proj[:, D_RNN:]
    conv = cb_ref[...]
    for k in range(CONV_WIDTH):
        start = SUBLANES - (CONV_WIDTH - 1) + k
        conv = conv + ext_ref[start:start + ts, :] * cw_ref[k:k + 1, :]
    ext_ref[0:SUBLANES, :] = ext_ref[ts:ts + SUBLANES, :]

    lam = lam_ref[...]
    neg = -lam
    softplus = jnp.maximum(neg, 0.0) + jnp.log1p(jnp.exp(-jnp.abs(neg)))
    conv_bf = conv.astype(BF16)
    for n in range(N_LRU_BLOCKS):
        sl = slice(n * LRU_BLOCK, (n + 1) * LRU_BLOCK)
        gax = jnp.dot(conv_bf[:, sl], wax_ref[n], preferred_element_type=F32)
        r = jax.nn.sigmoid(gax[:, :LRU_BLOCK] + ba_ref[:, sl])
        i = jax.nn.sigmoid(gax[:, LRU_BLOCK:] + bx_ref[:, sl])
        log_a = -LRU_C * r * softplus[:, sl]
        a = jnp.exp(log_a)
        a_ref[:, sl] = a
        one_minus_a2 = -jnp.tanh(log_a) * (a * a + 1.0)
        b_ref[:, sl] = jnp.sqrt(one_minus_a2) * (i * conv[:, sl])

    row = lax.broadcasted_iota(jnp.int32, (SUBLANES, D_RNN), 0)

    def group(gi, carry):
        off = pl.multiple_of(gi * SUBLANES, SUBLANES)
        a = a_ref[pl.ds(off, SUBLANES), :]
        b = b_ref[pl.ds(off, SUBLANES), :]
        for sh in (1, 2, 4):
            keep = row >= sh
            a_prev = jnp.where(keep, pltpu.roll(a, sh, 0), 1.0)
            b_prev = jnp.where(keep, pltpu.roll(b, sh, 0), 0.0)
            b = a * b_prev + b
            a = a * a_prev
        hh = a * carry + b
        b_ref[pl.ds(off, SUBLANES), :] = hh
        return jnp.broadcast_to(hh[SUBLANES - 1:SUBLANES, :], (SUBLANES, D_RNN))

    carry_ref[...] = lax.fori_loop(0, ts // SUBLANES, group, carry_ref[...], unroll=2)

    y = (jax.nn.gelu(gate, approximate=True) * b_ref[...]).astype(BF16)
    o_ref[0] = x + jnp.dot(y, wout_ref[...], preferred_element_type=F32)


def _rec(x3d, g, win, cw, cb, wax, ba, bx, lam, wout):
    b, s, _ = x3d.shape
    seq_spec = pl.BlockSpec((1, SEQ_ROWS, D_MODEL), lambda bi, si: (bi, si, 0))
    vec = _const_spec((1, D_RNN))
    return pl.pallas_call(
        _rec_kernel,
        out_shape=jax.ShapeDtypeStruct((b, s, D_MODEL), F32),
        grid=(b, s // SEQ_ROWS),
        in_specs=[seq_spec, _const_spec((1, D_MODEL)), _const_spec((D_MODEL, 2 * D_RNN)),
                  _const_spec((CONV_WIDTH, D_RNN)), vec,
                  _const_spec((N_LRU_BLOCKS, LRU_BLOCK, 2 * LRU_BLOCK)), vec, vec, vec,
                  _const_spec((D_RNN, D_MODEL))],
        out_specs=seq_spec,
        scratch_shapes=[pltpu.VMEM((SEQ_ROWS + SUBLANES, D_RNN), F32),
                        pltpu.VMEM((SEQ_ROWS, D_RNN), F32),
                        pltpu.VMEM((SEQ_ROWS, D_RNN), F32),
                        pltpu.VMEM((SUBLANES, D_RNN), F32)],
        compiler_params=_params(("arbitrary", "arbitrary")),
        name="rec_block",
    )(x3d, g, win, cw, cb, wax, ba, bx, lam, wout)


def _head_norm_rope(t, hn, cos, sin_signed, first_half, ones_blk):
    sq = t * t
    hi = sq.astype(BF16)
    lo = (sq - hi.astype(F32)).astype(BF16)
    ssum = (jnp.dot(hi, ones_blk, preferred_element_type=F32)
            + jnp.dot(lo, ones_blk, preferred_element_type=F32))
    tn = t * lax.rsqrt(ssum * (1.0 / HEAD_DIM) + EPS) * hn
    partner = jnp.where(first_half, pltpu.roll(tn, LANES - HEAD_DIM // 2, 1),
                        pltpu.roll(tn, HEAD_DIM // 2, 1))
    return tn * cos + partner * sin_signed


def _rope_consts(rows):
    lane = lax.broadcasted_iota(jnp.int32, (rows, LANES), 1)
    first_half = (lane % HEAD_DIM) < (HEAD_DIM // 2)
    r = lax.broadcasted_iota(jnp.int32, (LANES, LANES), 0) // HEAD_DIM
    c = lax.broadcasted_iota(jnp.int32, (LANES, LANES), 1) // HEAD_DIM
    ones_blk = jnp.where(r == c, 1.0, 0.0).astype(BF16)
    return first_half, ones_blk


def _kv_kernel(x_ref, g_ref, wk_ref, wv_ref, hn_ref, cos_ref, sin_ref, kt_ref, v_ref):
    x = x_ref[0]
    h = _rms(x, g_ref[...], EPS).astype(BF16)
    k = jnp.dot(h, wk_ref[...], preferred_element_type=F32)
    v = jnp.dot(h, wv_ref[...], preferred_element_type=F32)
    first_half, ones_blk = _rope_consts(SEQ_ROWS)
    cos = cos_ref[...]
    sin_signed = sin_ref[...]
    hn = hn_ref[...]
    for c in range(N_HEADS):
        sl = slice(c * LANES, (c + 1) * LANES)
        kr = _head_norm_rope(k[:, sl], hn, cos, sin_signed, first_half, ones_blk)
        kt_ref[0, c, 0] = kr.T.astype(BF16)
        v_ref[0, c] = v[:, sl].astype(BF16)


def _kv(x3d, g, wk, wv, hn, cos, sin_signed):
    b, s, _ = x3d.shape
    nt = s // SEQ_ROWS
    seq_spec = pl.BlockSpec((1, SEQ_ROWS, D_MODEL), lambda bi, si: (bi, si, 0))
    tab = pl.BlockSpec((SEQ_ROWS, LANES), lambda bi, si: (si, 0))
    return pl.pallas_call(
        _kv_kernel,
        out_shape=(jax.ShapeDtypeStruct((b, N_HEADS, nt, LANES, SEQ_ROWS), BF16),
                   jax.ShapeDtypeStruct((b, N_HEADS, s, LANES), BF16)),
        grid=(b, nt),
        in_specs=[seq_spec, _const_spec((1, D_MODEL)), _const_spec((D_MODEL, D_MODEL)),
                  _const_spec((D_MODEL, D_MODEL)), _const_spec((1, LANES)), tab, tab],
        out_specs=(pl.BlockSpec((1, N_HEADS, 1, LANES, SEQ_ROWS), lambda bi, si: (bi, 0, si, 0, 0)),
                   pl.BlockSpec((1, N_HEADS, SEQ_ROWS, LANES), lambda bi, si: (bi, 0, si, 0))),
        compiler_params=_params(("arbitrary", "arbitrary")),
        name="kv_proj",
    )(x3d, g, wk, wv, hn, cos, sin_signed)


def _q_kernel(x_ref, g_ref, wq_ref, hn_ref, cos_ref, sin_ref, q_ref):
    x = x_ref[0]
    h = _rms(x, g_ref[...], EPS).astype(BF16)
    q = jnp.dot(h, wq_ref[...], preferred_element_type=F32)
    first_half, ones_blk = _rope_consts(SEQ_ROWS)
    cos = cos_ref[...]
    sin_signed = sin_ref[...]
    hn = hn_ref[...]
    for c in range(N_HEADS):
        sl = slice(c * LANES, (c + 1) * LANES)
        qr = _head_norm_rope(q[:, sl], hn, cos, sin_signed, first_half, ones_blk)
        q_ref[0, c] = (qr * (HEAD_DIM ** -0.5 * LOG2E)).astype(BF16)


def _qproj(x3d, g, wq, hn, cos, sin_signed):
    b, s, _ = x3d.shape
    seq_spec = pl.BlockSpec((1, SEQ_ROWS, D_MODEL), lambda bi, si: (bi, si, 0))
    tab = pl.BlockSpec((SEQ_ROWS, LANES), lambda bi, si: (si, 0))
    return pl.pallas_call(
        _q_kernel,
        out_shape=jax.ShapeDtypeStruct((b, N_HEADS, s, LANES), BF16),
        grid=(b, s // SEQ_ROWS),
        in_specs=[seq_spec, _const_spec((1, D_MODEL)), _const_spec((D_MODEL, D_MODEL)),
                  _const_spec((1, LANES)), tab, tab],
        out_specs=pl.BlockSpec((1, N_HEADS, SEQ_ROWS, LANES), lambda bi, si: (bi, 0, si, 0)),
        compiler_params=_params(("arbitrary", "arbitrary")),
        name="q_proj",
    )(x3d, g, wq, hn, cos, sin_signed)


def _attn_kernel(q_ref, kt_ref, v_ref, lq1_ref, lk1_ref, lq2_ref, lk2_ref, sn_ref, o_ref,
                 s_ref, p_ref, m_ref, l_ref, acc_ref, *, lambda_init):
    qi = pl.program_id(2)
    lane = lax.broadcasted_iota(jnp.int32, (ATT_Q, LANES), 1)
    qs = []
    for hh in range(ATT_HEADS):
        q = q_ref[0, hh]
        zero = jnp.zeros_like(q)
        qs += [(hh, jnp.where(lane < HEAD_DIM, q, zero)),
               (hh, jnp.where(lane >= HEAD_DIM, q, zero))]
    n_chain = len(qs)

    m_ref[...] = jnp.full(m_ref.shape, -jnp.inf, F32)
    l_ref[...] = jnp.zeros(l_ref.shape, F32)
    acc_ref[...] = jnp.zeros(acc_ref.shape, F32)

    def kv_tile(j, diagonal):
        for c, (hh, qc) in enumerate(qs):
            s_ref[c] = jnp.dot(qc, kt_ref[0, hh, j], preferred_element_type=F32)
        alphas = []
        for c in range(n_chain):
            if diagonal:
                q_chunk = lax.broadcasted_iota(jnp.int32, (ATT_Q, ATT_K), 0) // CHUNK
                k_chunk = lax.broadcasted_iota(jnp.int32, (ATT_Q, ATT_K), 1) // CHUNK
                s_all = jnp.where(k_chunk <= q_chunk, s_ref[c], -jnp.inf)
            else:
                s_all = s_ref[c]
            m_cur = jnp.max(s_all, axis=-1, keepdims=True)
            m_old = m_ref[c]
            m_new = jnp.maximum(m_old, jnp.broadcast_to(m_cur, (ATT_Q, LANES)))
            alpha = jnp.exp2(m_old - m_new)
            m_ref[c] = m_new
            alphas.append(alpha)
            for rb in range(ATT_Q // ATT_ROWS):
                rows = slice(rb * ATT_ROWS, (rb + 1) * ATT_ROWS)
                ncols = ((rb * ATT_ROWS) // CHUNK + 1) * CHUNK if diagonal else ATT_K
                mb = m_new[rows, :]
                psum = alpha[rows, :] * l_ref[c, rows, :]
                for k in range(ATT_K // LANES):
                    cols = slice(k * LANES, (k + 1) * LANES)
                    if k * LANES >= ncols:
                        p_ref[c, rows, cols] = jnp.zeros((ATT_ROWS, LANES), BF16)
                        continue
                    p = jnp.exp2(s_ref[c, rows, cols] - mb)
                    if (k + 1) * LANES > ncols:
                        lane_k = lax.broadcasted_iota(jnp.int32, (ATT_ROWS, LANES), 1)
                        p = jnp.where(lane_k < ncols - k * LANES, p, 0.0)
                    psum = psum + p
                    p_ref[c, rows, cols] = p.astype(BF16)
                l_ref[c, rows, :] = psum
        for c, (hh, _) in enumerate(qs):
            vv = v_ref[0, hh, pl.ds(pl.multiple_of(j * ATT_K, ATT_K), ATT_K), :]
            acc_ref[c] = alphas[c] * acc_ref[c] + jnp.dot(p_ref[c], vv,
                                                          preferred_element_type=F32)

    def full_tile(j, carry):
        kv_tile(j, False)
        return carry

    lax.fori_loop(0, qi, full_tile, 0)
    kv_tile(qi, True)

    lam = (jnp.exp(jnp.sum(lq1_ref[...] * lk1_ref[...], axis=-1, keepdims=True))
           - jnp.exp(jnp.sum(lq2_ref[...] * lk2_ref[...], axis=-1, keepdims=True))
           + lambda_init)
    for hh in range(ATT_HEADS):
        c1, c2 = 2 * hh, 2 * hh + 1
        l1 = jnp.sum(l_ref[c1], axis=-1, keepdims=True)
        l2 = jnp.sum(l_ref[c2], axis=-1, keepdims=True)
        o = acc_ref[c1] / l1 - lam * (acc_ref[c2] / l2)
        o = _rms(o, sn_ref[...], SUBLN_EPS) * (1.0 - lambda_init)
        o_ref[0, :, hh * LANES:(hh + 1) * LANES] = o.astype(BF16)


def _attn(q, kt, v, lq1, lk1, lq2, lk2, sn, lambda_init):
    b, nh, s, _ = q.shape
    nt = s // ATT_K
    vec = _const_spec((1, HEAD_DIM))
    return pl.pallas_call(
        functools.partial(_attn_kernel, lambda_init=lambda_init),
        out_shape=jax.ShapeDtypeStruct((b, s, D_MODEL), BF16),
        grid=(b, nh // ATT_HEADS, s // ATT_Q),
        in_specs=[pl.BlockSpec((1, ATT_HEADS, ATT_Q, LANES), lambda bi, hi, qi: (bi, hi, qi, 0)),
                  pl.BlockSpec((1, ATT_HEADS, nt, LANES, ATT_K),
                               lambda bi, hi, qi: (bi, hi, 0, 0, 0)),
                  pl.BlockSpec((1, ATT_HEADS, s, LANES), lambda bi, hi, qi: (bi, hi, 0, 0)),
                  vec, vec, vec, vec, _const_spec((1, LANES))],
        out_specs=pl.BlockSpec((1, ATT_Q, ATT_HEADS * LANES), lambda bi, hi, qi: (bi, qi, hi)),
        scratch_shapes=[pltpu.VMEM((2 * ATT_HEADS, ATT_Q, ATT_K), F32),
                        pltpu.VMEM((2 * ATT_HEADS, ATT_Q, ATT_K), BF16),
                        pltpu.VMEM((2 * ATT_HEADS, ATT_Q, LANES), F32),
                        pltpu.VMEM((2 * ATT_HEADS, ATT_Q, LANES), F32),
                        pltpu.VMEM((2 * ATT_HEADS, ATT_Q, LANES), F32)],
        compiler_params=_params(("arbitrary", "arbitrary", "arbitrary")),
        name="diff_attn",
    )(q, kt, v, lq1, lk1, lq2, lk2, sn)


def _oproj_kernel(x_ref, o_ref, w_ref, out_ref):
    out_ref[...] = x_ref[...] + jnp.dot(o_ref[...], w_ref[...], preferred_element_type=F32)


def _oproj(x2d, o2d, w):
    m = x2d.shape[0]
    row_spec = pl.BlockSpec((FFN_ROWS, D_MODEL), lambda i: (i, 0))
    return pl.pallas_call(
        _oproj_kernel,
        out_shape=jax.ShapeDtypeStruct((m, D_MODEL), F32),
        grid=(m // FFN_ROWS,),
        in_specs=[row_spec, row_spec, _const_spec((D_MODEL, D_MODEL))],
        out_specs=row_spec,
        compiler_params=_params(("arbitrary",)),
        name="attn_out_proj",
    )(x2d, o2d, w)


def _rope_tables(s):
    pos = jnp.arange(s, dtype=F32)
    inv_freq = ROPE_THETA ** (-jnp.arange(0, HEAD_DIM, 2, dtype=F32) / HEAD_DIM)
    ang = pos[:, None] * inv_freq[None, :]
    ang = jnp.concatenate([ang, ang], axis=-1)
    cos, sin = jnp.cos(ang), jnp.sin(ang)
    half = HEAD_DIM // 2
    sin_signed = jnp.concatenate([-sin[:, :half], sin[:, half:]], axis=-1)
    reps = LANES // HEAD_DIM
    return jnp.tile(cos, (1, reps)), jnp.tile(sin_signed, (1, reps))


def kernel(x, ffn1_norm, ffn1_w_gate, ffn1_w_up, ffn1_w_down, ffn2_norm, ffn2_w_gate, ffn2_w_up, ffn2_w_down, mix_norm, rec_w_in, rec_conv_w, rec_conv_b, rec_w_a, rec_b_a, rec_w_x, rec_b_x, rec_lambda, rec_w_out, kv_norm, w_k, w_v, k_norm, lambda_k1, lambda_k2, attn_w_q, q_norm, lambda_q1, lambda_q2, sub_norm, attn_w_o):
    b, s, d = x.shape
    m = b * s
    reps = LANES // HEAD_DIM
    cos, sin_signed = _rope_tables(s)

    f1 = (ffn1_w_gate.astype(BF16), ffn1_w_up.astype(BF16), ffn1_w_down.astype(BF16))
    f2 = (ffn2_w_gate.astype(BF16), ffn2_w_up.astype(BF16), ffn2_w_down.astype(BF16))
    w_in = rec_w_in.astype(BF16)
    w_ax = jnp.concatenate([rec_w_a, rec_w_x], axis=-1).astype(BF16)
    w_out = rec_w_out.astype(BF16)
    wk, wv = w_k.astype(BF16), w_v.astype(BF16)
    wq, wo = attn_w_q.astype(BF16), attn_w_o.astype(BF16)

    def row(vec):
        return vec.reshape(1, -1)

    kt = v = None
    for layer in range(DEPTH):
        if layer == N_A_LAYERS:
            kt, v = _kv(x, row(kv_norm), wk, wv, row(jnp.tile(k_norm, reps)), cos, sin_signed)
        x = _ffn(x.reshape(m, d), row(ffn1_norm[layer]), f1[0][layer], f1[1][layer],
                 f1[2][layer]).reshape(b, s, d)
        if layer < N_A_LAYERS:
            a = layer
            x = _rec(x, row(mix_norm[layer]), w_in[a], rec_conv_w[a], row(rec_conv_b[a]), w_ax[a],
                     row(rec_b_a[a]), row(rec_b_x[a]), row(rec_lambda[a]), w_out[a])
        else:
            j = layer - N_A_LAYERS
            lambda_init = 0.8 - 0.6 * math.exp(-0.3 * layer)
            q = _qproj(x, row(mix_norm[layer]), wq[j], row(jnp.tile(q_norm[j], reps)), cos,
                       sin_signed)
            o = _attn(q, kt, v, row(lambda_q1[j]), row(lambda_k1), row(lambda_q2[j]),
                      row(lambda_k2), row(sub_norm[j]), lambda_init)
            x = _oproj(x.reshape(m, d), o.reshape(m, d), wo[j]).reshape(b, s, d)
        x = _ffn(x.reshape(m, d), row(ffn2_norm[layer]), f2[0][layer], f2[1][layer],
                 f2[2][layer]).reshape(b, s, d)
    return x
```

```python
import functools
import math

import jax
import jax.numpy as jnp
from jax import lax
from jax.experimental import pallas as pl
from jax.experimental.pallas import tpu as pltpu

F32 = jnp.float32
BF16 = jnp.bfloat16

D_MODEL = 1024
D_FF = 2816
DEPTH = 4
N_A_LAYERS = DEPTH // 2
CHUNK = 64
D_RNN = D_MODEL
N_LRU_BLOCKS = 8
LRU_BLOCK = D_RNN // N_LRU_BLOCKS
CONV_WIDTH = 4
LRU_C = 8.0
N_HEADS = 8
HEAD_DIM = 64
ROPE_THETA = 10000.0
EPS = 1e-6
SUBLN_EPS = 1e-5

LANES = 128
SUBLANES = 8
VMEM_LIMIT = 56 * 1024 * 1024

FFN_ROWS = 512
FFN_COLS = 256
SEQ_ROWS = 512
PROJ_COLS = 256
ATT_Q = 512
ATT_K = 512
ATT_HEADS = 2
ATT_ROWS = 32
VT_ROWS = 144
LOG2E = 1.4426950408889634


def _rms(xf, g, eps):
    ms = jnp.mean(xf * xf, axis=-1, keepdims=True)
    return xf * lax.rsqrt(ms + eps) * g


def _params(sem):
    return pltpu.CompilerParams(dimension_semantics=sem, vmem_limit_bytes=VMEM_LIMIT)


def _const_spec(shape):
    nd = len(shape)
    return pl.BlockSpec(shape, lambda *_: (0,) * nd, pipeline_mode=pl.Buffered(1))


def _layer_spec(shape, layer):
    nd = len(shape)
    return pl.BlockSpec((None,) + tuple(shape), lambda *_: (layer,) + (0,) * nd,
                        pipeline_mode=pl.Buffered(1))


def _ffn_kernel(*refs, with_attn):
    if with_attn:
        x_ref, ao_ref, wo_ref, g_ref, wg_ref, wu_ref, wd_ref, o_ref, act_ref = refs
        x = x_ref[...] + jnp.dot(ao_ref[...], wo_ref[...], preferred_element_type=F32)
    else:
        x_ref, g_ref, wg_ref, wu_ref, wd_ref, o_ref, act_ref = refs
        x = x_ref[...]
    h = _rms(x, g_ref[...], EPS).astype(BF16)
    for c in range(D_FF // FFN_COLS):
        sl = slice(c * FFN_COLS, (c + 1) * FFN_COLS)
        gate = jnp.dot(h, wg_ref[:, sl], preferred_element_type=F32)
        up = jnp.dot(h, wu_ref[:, sl], preferred_element_type=F32)
        act_ref[:, sl] = (gate * jax.nn.sigmoid(gate) * up).astype(BF16)
    y = jnp.dot(act_ref[...], wd_ref[...], preferred_element_type=F32)
    o_ref[...] = x + 0.5 * y


def _ffn(x2d, layer, g, wg, wu, wd, attn=None):
    m = x2d.shape[0]
    row_spec = pl.BlockSpec((FFN_ROWS, D_MODEL), lambda i: (i, 0))
    operands, in_specs = [x2d], [row_spec]
    if attn is not None:
        o2d, wo, j = attn
        operands += [o2d, wo]
        in_specs += [row_spec, _layer_spec((D_MODEL, D_MODEL), j)]
    operands += [g, wg, wu, wd]
    in_specs += [_layer_spec((1, D_MODEL), layer), _layer_spec((D_MODEL, D_FF), layer),
                 _layer_spec((D_MODEL, D_FF), layer), _layer_spec((D_FF, D_MODEL), layer)]
    return pl.pallas_call(
        functools.partial(_ffn_kernel, with_attn=attn is not None),
        out_shape=jax.ShapeDtypeStruct((m, D_MODEL), F32),
        grid=(m // FFN_ROWS,),
        in_specs=in_specs,
        out_specs=row_spec,
        scratch_shapes=[pltpu.VMEM((FFN_ROWS, D_FF), BF16)],
        compiler_params=_params(("arbitrary",)),
        name="ffn_attn_out" if attn is not None else "ffn",
    )(*operands)


def _rec_kernel(x_ref, g_ref, win_ref, cw_ref, cb_ref, wax_ref, ba_ref, bx_ref, lam_ref,
                wout_ref, o_ref, ext_ref, a_ref, b_ref, carry_ref):
    ts = SEQ_ROWS

    @pl.when(pl.program_id(1) == 0)
    def _():
        ext_ref[0:SUBLANES, :] = jnp.zeros((SUBLANES, D_RNN), F32)
        carry_ref[...] = jnp.zeros((SUBLANES, D_RNN), F32)

    x = x_ref[0]
    h = _rms(x, g_ref[...], EPS).astype(BF16)
    proj = jnp.dot(h, win_ref[...], preferred_element_type=F32)
    gate = proj[:, :D_RNN]

    ext_ref[SUBLANES:SUBLANES + ts, :] = proj[:, D_RNN:]
    conv = cb_ref[...]
    for k in range(CONV_WIDTH):
        start = SUBLANES - (CONV_WIDTH - 1) + k
        conv = conv + ext_ref[start:start + ts, :] * cw_ref[k:k + 1, :]
    ext_ref[0:SUBLANES, :] = ext_ref[ts:ts + SUBLANES, :]

    lam = lam_ref[...]
    neg = -lam
    softplus = jnp.maximum(neg, 0.0) + jnp.log1p(jnp.exp(-jnp.abs(neg)))
    conv_bf = conv.astype(BF16)
    for n in range(N_LRU_BLOCKS):
        sl = slice(n * LRU_BLOCK, (n + 1) * LRU_BLOCK)
        gax = jnp.dot(conv_bf[:, sl], wax_ref[n], preferred_element_type=F32)
        r = jax.nn.sigmoid(gax[:, :LRU_BLOCK] + ba_ref[:, sl])
        i = jax.nn.sigmoid(gax[:, LRU_BLOCK:] + bx_ref[:, sl])
        log_a = -LRU_C * r * softplus[:, sl]
        a = jnp.exp(log_a)
        a_ref[:, sl] = a
        one_minus_a2 = -jnp.tanh(log_a) * (a * a + 1.0)
        b_ref[:, sl] = jnp.sqrt(one_minus_a2) * (i * conv[:, sl])

    row = lax.broadcasted_iota(jnp.int32, (SUBLANES, D_RNN), 0)

    def group(gi, carry):
        off = pl.multiple_of(gi * SUBLANES, SUBLANES)
        a = a_ref[pl.ds(off, SUBLANES), :]
        b = b_ref[pl.ds(off, SUBLANES), :]
        for sh in (1, 2, 4):
            keep = row >= sh
            a_prev = jnp.where(keep, pltpu.roll(a, sh, 0), 1.0)
            b_prev = jnp.where(keep, pltpu.roll(b, sh, 0), 0.0)
            b = a * b_prev + b
            a = a * a_prev
        hh = a * carry + b
        b_ref[pl.ds(off, SUBLANES), :] = hh
        return jnp.broadcast_to(hh[SUBLANES - 1:SUBLANES, :], (SUBLANES, D_RNN))

    carry_ref[...] = lax.fori_loop(0, ts // SUBLANES, group, carry_ref[...], unroll=2)

    y = (jax.nn.gelu(gate, approximate=True) * b_ref[...]).astype(BF16)
    o_ref[0] = x + jnp.dot(y, wout_ref[...], preferred_element_type=F32)


def _rec(x3d, layer, a, g, win, cw, cb, wax, ba, bx, lam, wout):
    b, s, _ = x3d.shape
    seq_spec = pl.BlockSpec((1, SEQ_ROWS, D_MODEL), lambda bi, si: (bi, si, 0))
    vec = _layer_spec((1, D_RNN), a)
    return pl.pallas_call(
        _rec_kernel,
        out_shape=jax.ShapeDtypeStruct((b, s, D_MODEL), F32),
        grid=(b, s // SEQ_ROWS),
        in_specs=[seq_spec, _layer_spec((1, D_MODEL), layer),
                  _layer_spec((D_MODEL, 2 * D_RNN), a),
                  _layer_spec((CONV_WIDTH, D_RNN), a), vec,
                  _layer_spec((N_LRU_BLOCKS, LRU_BLOCK, 2 * LRU_BLOCK), a), vec, vec, vec,
                  _layer_spec((D_RNN, D_MODEL), a)],
        out_specs=seq_spec,
        scratch_shapes=[pltpu.VMEM((SEQ_ROWS + SUBLANES, D_RNN), F32),
                        pltpu.VMEM((SEQ_ROWS, D_RNN), F32),
                        pltpu.VMEM((SEQ_ROWS, D_RNN), F32),
                        pltpu.VMEM((SUBLANES, D_RNN), F32)],
        compiler_params=_params(("arbitrary", "arbitrary")),
        name="rec_block",
    )(x3d, g, win, cw, cb, wax, ba, bx, lam, wout)


def _head_norm_rope(t, hn, cos, sin_signed, first_half, ones_blk):
    sq = t * t
    hi = sq.astype(BF16)
    lo = (sq - hi.astype(F32)).astype(BF16)
    ssum = jnp.dot(jnp.concatenate([hi, lo], axis=1), ones_blk, preferred_element_type=F32)
    tn = t * lax.rsqrt(ssum * (1.0 / HEAD_DIM) + EPS) * hn
    partner = jnp.where(first_half, pltpu.roll(tn, LANES - HEAD_DIM // 2, 1),
                        pltpu.roll(tn, HEAD_DIM // 2, 1))
    return tn * cos + partner * sin_signed


def _rope_consts(rows):
    lane = lax.broadcasted_iota(jnp.int32, (rows, LANES), 1)
    first_half = (lane % HEAD_DIM) < (HEAD_DIM // 2)
    r = (lax.broadcasted_iota(jnp.int32, (2 * LANES, LANES), 0) % LANES) // HEAD_DIM
    c = lax.broadcasted_iota(jnp.int32, (2 * LANES, LANES), 1) // HEAD_DIM
    ones_blk = jnp.where(r == c, 1.0, 0.0).astype(BF16)
    return first_half, ones_blk


def _kv_kernel(x_ref, g_ref, wk_ref, wv_ref, hn_ref, cos_ref, sin_ref, k_ref, vt_ref):
    x = x_ref[0]
    h = _rms(x, g_ref[...], EPS).astype(BF16)
    first_half, ones_blk = _rope_consts(SEQ_ROWS)
    cos = cos_ref[...]
    sin_signed = sin_ref[...]
    hn = hn_ref[...]
    for cc in range(D_MODEL // PROJ_COLS):
        cols = slice(cc * PROJ_COLS, (cc + 1) * PROJ_COLS)
        k = jnp.dot(h, wk_ref[:, cols], preferred_element_type=F32)
        v = jnp.dot(h, wv_ref[:, cols], preferred_element_type=F32)
        for i in range(PROJ_COLS // LANES):
            c = cc * (PROJ_COLS // LANES) + i
            sl = slice(i * LANES, (i + 1) * LANES)
            kr = _head_norm_rope(k[:, sl], hn, cos, sin_signed, first_half, ones_blk)
            k_ref[0, c] = kr.astype(BF16)
            vt_ref[0, c, 0, :LANES, :] = v[:, sl].T.astype(BF16)
            pad_row = lax.broadcasted_iota(jnp.int32, (VT_ROWS - LANES, SEQ_ROWS), 0)
            vt_ref[0, c, 0, LANES:, :] = jnp.where(pad_row == 0, 1.0, 0.0).astype(BF16)


def _kv(x3d, g, wk, wv, hn, cos, sin_signed):
    b, s, _ = x3d.shape
    nt = s // SEQ_ROWS
    seq_spec = pl.BlockSpec((1, SEQ_ROWS, D_MODEL), lambda bi, si: (bi, si, 0))
    tab = pl.BlockSpec((SEQ_ROWS, LANES), lambda bi, si: (si, 0))
    return pl.pallas_call(
        _kv_kernel,
        out_shape=(jax.ShapeDtypeStruct((b, N_HEADS, s, LANES), BF16),
                   jax.ShapeDtypeStruct((b, N_HEADS, nt, VT_ROWS, SEQ_ROWS), BF16)),
        grid=(b, nt),
        in_specs=[seq_spec, _const_spec((1, D_MODEL)), _const_spec((D_MODEL, D_MODEL)),
                  _const_spec((D_MODEL, D_MODEL)), _const_spec((1, LANES)), tab, tab],
        out_specs=(pl.BlockSpec((1, N_HEADS, SEQ_ROWS, LANES), lambda bi, si: (bi, 0, si, 0)),
                   pl.BlockSpec((1, N_HEADS, 1, VT_ROWS, SEQ_ROWS),
                                lambda bi, si: (bi, 0, si, 0, 0))),
        compiler_params=_params(("arbitrary", "arbitrary")),
        name="kv_proj",
    )(x3d, g, wk, wv, hn, cos, sin_signed)


def _q_kernel(x_ref, g_ref, wq_ref, hn_ref, cos_ref, sin_ref, qt_ref):
    x = x_ref[0]
    h = _rms(x, g_ref[...], EPS).astype(BF16)
    first_half, ones_blk = _rope_consts(SEQ_ROWS)
    cos = cos_ref[...]
    sin_signed = sin_ref[...]
    hn = hn_ref[...]

    def project(cc):
        return jnp.dot(h, wq_ref[:, cc * PROJ_COLS:(cc + 1) * PROJ_COLS],
                       preferred_element_type=F32)

    n_chunks = D_MODEL // PROJ_COLS
    q_next = project(0)
    for cc in range(n_chunks):
        q = q_next
        if cc + 1 < n_chunks:
            q_next = project(cc + 1)
        for i in range(PROJ_COLS // LANES):
            c = cc * (PROJ_COLS // LANES) + i
            qr = _head_norm_rope(q[:, i * LANES:(i + 1) * LANES], hn, cos, sin_signed, first_half,
                                 ones_blk)
            qt_ref[0, c, 0] = (qr * (HEAD_DIM ** -0.5 * LOG2E)).T.astype(BF16)


def _qproj(x3d, layer, j, g, wq, hn, cos, sin_signed):
    b, s, _ = x3d.shape
    nt = s // SEQ_ROWS
    seq_spec = pl.BlockSpec((1, SEQ_ROWS, D_MODEL), lambda bi, si: (bi, si, 0))
    tab = pl.BlockSpec((SEQ_ROWS, LANES), lambda bi, si: (si, 0))
    return pl.pallas_call(
        _q_kernel,
        out_shape=jax.ShapeDtypeStruct((b, N_HEADS, nt, LANES, SEQ_ROWS), BF16),
        grid=(b, nt),
        in_specs=[seq_spec, _layer_spec((1, D_MODEL), layer), _layer_spec((D_MODEL, D_MODEL), j),
                  _layer_spec((1, LANES), j), tab, tab],
        out_specs=pl.BlockSpec((1, N_HEADS, 1, LANES, SEQ_ROWS), lambda bi, si: (bi, 0, si, 0, 0)),
        compiler_params=_params(("arbitrary", "arbitrary")),
        name="q_proj",
    )(x3d, g, wq, hn, cos, sin_signed)


def _attn_kernel(qt_ref, k_ref, vt_ref, lq1_ref, lk1_ref, lq2_ref, lk2_ref, sn_ref, o_ref,
                 s_ref, p_ref, m_ref, acc_ref, *, lambda_init):
    qi = pl.program_id(2)
    chains = []
    for hh in range(ATT_HEADS):
        qt = qt_ref[0, hh, 0]
        zero = jnp.zeros((HEAD_DIM, ATT_Q), BF16)
        chains += [(hh, jnp.concatenate([qt[:HEAD_DIM], zero], axis=0)),
                   (hh, jnp.concatenate([zero, qt[HEAD_DIM:]], axis=0))]
    n_chain = len(chains)
    n_lane_groups = ATT_Q // LANES

    m_ref[...] = jnp.full(m_ref.shape, -jnp.inf, F32)
    acc_ref[...] = jnp.zeros(acc_ref.shape, F32)

    def kv_tile(j, diagonal):
        def scores(c):
            hh, qz = chains[c]
            kj = k_ref[0, hh, pl.ds(pl.multiple_of(j * ATT_K, ATT_K), ATT_K), :]
            s_ref[c] = jnp.dot(kj, qz, preferred_element_type=F32)

        scores(0)
        scores(1)
        for c in range(n_chain):
            if diagonal:
                k_chunk = lax.broadcasted_iota(jnp.int32, (ATT_K, ATT_Q), 0) // CHUNK
                q_chunk = lax.broadcasted_iota(jnp.int32, (ATT_K, ATT_Q), 1) // CHUNK
                s_all = jnp.where(k_chunk <= q_chunk, s_ref[c], -jnp.inf)
            else:
                s_all = s_ref[c]
            m_old = m_ref[c]
            m_new = jnp.maximum(m_old, jnp.max(s_all, axis=0, keepdims=True))
            alpha = jnp.exp2(m_old - m_new)
            m_ref[c] = m_new
            for rb in range(ATT_K // ATT_ROWS):
                rows = slice(rb * ATT_ROWS, (rb + 1) * ATT_ROWS)
                first_col = ((rb * ATT_ROWS) // CHUNK) * CHUNK if diagonal else 0
                for g in range(n_lane_groups):
                    cols = slice(g * LANES, (g + 1) * LANES)
                    if (g + 1) * LANES <= first_col:
                        p_ref[c, rows, cols] = jnp.zeros((ATT_ROWS, LANES), BF16)
                        continue
                    mb = jnp.tile(m_new[:, cols], (ATT_ROWS // SUBLANES, 1))
                    p = jnp.exp2(s_ref[c, rows, cols] - mb)
                    if g * LANES < first_col:
                        lane = lax.broadcasted_iota(jnp.int32, (ATT_ROWS, LANES), 1)
                        p = jnp.where(lane >= first_col - g * LANES, p, 0.0)
                    p_ref[c, rows, cols] = p.astype(BF16)
            if c + 2 < n_chain:
                scores(c + 2)
            scale = jnp.tile(alpha, (VT_ROWS // SUBLANES, 1))
            acc_ref[c] = scale * acc_ref[c] + jnp.dot(vt_ref[0, chains[c][0], j], p_ref[c],
                                                      preferred_element_type=F32)

    def full_tile(j, carry):
        kv_tile(j, False)
        return carry

    lax.fori_loop(0, qi, full_tile, 0)
    kv_tile(qi, True)

    lam = (jnp.exp(jnp.sum(lq1_ref[...] * lk1_ref[...], axis=-1, keepdims=True))
           - jnp.exp(jnp.sum(lq2_ref[...] * lk2_ref[...], axis=-1, keepdims=True))
           + lambda_init)
    gain = jnp.tile(sn_ref[...], (1, n_lane_groups))
    for hh in range(ATT_HEADS):
        c1, c2 = 2 * hh, 2 * hh + 1
        inv_l1 = 1.0 / acc_ref[c1, LANES:LANES + 1, :]
        inv_l2 = 1.0 / acc_ref[c2, LANES:LANES + 1, :]
        ot = acc_ref[c1, :LANES, :] * inv_l1 - lam * (acc_ref[c2, :LANES, :] * inv_l2)
        ms = jnp.mean(ot * ot, axis=0, keepdims=True)
        yt = ot * lax.rsqrt(ms + SUBLN_EPS) * gain * (1.0 - lambda_init)
        o_ref[0, :, hh * LANES:(hh + 1) * LANES] = yt.T.astype(BF16)


def _attn(qt, k, vt, j, lq1, lk1, lq2, lk2, sn, lambda_init):
    b, nh, s, _ = k.shape
    nt = s // ATT_K
    vec = _const_spec((1, HEAD_DIM))
    lvec = _layer_spec((1, HEAD_DIM), j)
    n_chain = 2 * ATT_HEADS
    return pl.pallas_call(
        functools.partial(_attn_kernel, lambda_init=lambda_init),
        out_shape=jax.ShapeDtypeStruct((b, s, D_MODEL), BF16),
        grid=(b, nh // ATT_HEADS, s // ATT_Q),
        in_specs=[pl.BlockSpec((1, ATT_HEADS, 1, LANES, ATT_Q),
                               lambda bi, hi, qi: (bi, hi, qi, 0, 0)),
                  pl.BlockSpec((1, ATT_HEADS, s, LANES), lambda bi, hi, qi: (bi, hi, 0, 0)),
                  pl.BlockSpec((1, ATT_HEADS, nt, VT_ROWS, ATT_K),
                               lambda bi, hi, qi: (bi, hi, 0, 0, 0)),
                  lvec, vec, lvec, vec, _layer_spec((LANES, LANES), j)],
        out_specs=pl.BlockSpec((1, ATT_Q, ATT_HEADS * LANES), lambda bi, hi, qi: (bi, qi, hi)),
        scratch_shapes=[pltpu.VMEM((n_chain, ATT_K, ATT_Q), F32),
                        pltpu.VMEM((n_chain, ATT_K, ATT_Q), BF16),
                        pltpu.VMEM((n_chain, SUBLANES, ATT_Q), F32),
                        pltpu.VMEM((n_chain, VT_ROWS, ATT_Q), F32)],
        compiler_params=_params(("arbitrary", "arbitrary", "arbitrary")),
        name="diff_attn",
    )(qt, k, vt, lq1, lk1, lq2, lk2, sn)


def _rope_tables(s):
    pos = jnp.arange(s, dtype=F32)
    inv_freq = ROPE_THETA ** (-jnp.arange(0, HEAD_DIM, 2, dtype=F32) / HEAD_DIM)
    ang = pos[:, None] * inv_freq[None, :]
    ang = jnp.concatenate([ang, ang], axis=-1)
    cos, sin = jnp.cos(ang), jnp.sin(ang)
    half = HEAD_DIM // 2
    sin_signed = jnp.concatenate([-sin[:, :half], sin[:, half:]], axis=-1)
    reps = LANES // HEAD_DIM
    return jnp.tile(cos, (1, reps)), jnp.tile(sin_signed, (1, reps))


def kernel(x, ffn1_norm, ffn1_w_gate, ffn1_w_up, ffn1_w_down, ffn2_norm, ffn2_w_gate, ffn2_w_up, ffn2_w_down, mix_norm, rec_w_in, rec_conv_w, rec_conv_b, rec_w_a, rec_b_a, rec_w_x, rec_b_x, rec_lambda, rec_w_out, kv_norm, w_k, w_v, k_norm, lambda_k1, lambda_k2, attn_w_q, q_norm, lambda_q1, lambda_q2, sub_norm, attn_w_o):
    b, s, d = x.shape
    m = b * s
    reps = LANES // HEAD_DIM
    cos, sin_signed = _rope_tables(s)

    f1 = (ffn1_w_gate.astype(BF16), ffn1_w_up.astype(BF16), ffn1_w_down.astype(BF16))
    f2 = (ffn2_w_gate.astype(BF16), ffn2_w_up.astype(BF16), ffn2_w_down.astype(BF16))
    w_in = rec_w_in.astype(BF16)
    w_ax = jnp.concatenate([rec_w_a, rec_w_x], axis=-1).astype(BF16)
    w_out = rec_w_out.astype(BF16)
    wk, wv = w_k.astype(BF16), w_v.astype(BF16)
    wq, wo = attn_w_q.astype(BF16), attn_w_o.astype(BF16)

    def row(vec):
        return vec.reshape(1, -1)

    def rows(stacked):
        return stacked.reshape(stacked.shape[0], 1, stacked.shape[1])

    g1, g2, gmix = rows(ffn1_norm), rows(ffn2_norm), rows(mix_norm)
    conv_b, b_a, b_x, lam = rows(rec_conv_b), rows(rec_b_a), rows(rec_b_x), rows(rec_lambda)
    q_hn = rows(jnp.tile(q_norm, (1, reps)))
    lq1, lq2 = rows(lambda_q1), rows(lambda_q2)
    sub_g = jnp.broadcast_to(sub_norm[:, :, None], sub_norm.shape + (LANES,))

    k_shared = vt_shared = None
    x2 = x.reshape(m, d)
    for layer in range(DEPTH):
        if layer == N_A_LAYERS:
            k_shared, vt_shared = _kv(x2.reshape(b, s, d), row(kv_norm), wk, wv,
                                      row(jnp.tile(k_norm, reps)), cos, sin_signed)
        x2 = _ffn(x2, layer, g1, *f1)
        if layer < N_A_LAYERS:
            a = layer
            x2 = _rec(x2.reshape(b, s, d), layer, a, gmix, w_in, rec_conv_w, conv_b, w_ax, b_a,
                      b_x, lam, w_out).reshape(m, d)
            x2 = _ffn(x2, layer, g2, *f2)
        else:
            j = layer - N_A_LAYERS
            lambda_init = 0.8 - 0.6 * math.exp(-0.3 * layer)
            qt = _qproj(x2.reshape(b, s, d), layer, j, gmix, wq, q_hn, cos, sin_signed)
            o = _attn(qt, k_shared, vt_shared, j, lq1, row(lambda_k1), lq2, row(lambda_k2), sub_g,
                      lambda_init)
            x2 = _ffn(x2, layer, g2, *f2, attn=(o.reshape(m, d), wo, j))
    return x2.reshape(b, s, d)
```

```python
import functools
import math

import jax
import jax.numpy as jnp
from jax import lax
from jax.experimental import pallas as pl
from jax.experimental.pallas import tpu as pltpu

F32 = jnp.float32
BF16 = jnp.bfloat16

D_MODEL = 1024
D_FF = 2816
DEPTH = 4
N_A_LAYERS = DEPTH // 2
CHUNK = 64
D_RNN = D_MODEL
N_LRU_BLOCKS = 8
LRU_BLOCK = D_RNN // N_LRU_BLOCKS
CONV_WIDTH = 4
LRU_C = 8.0
N_HEADS = 8
HEAD_DIM = 64
ROPE_THETA = 10000.0
EPS = 1e-6
SUBLN_EPS = 1e-5

LANES = 128
SUBLANES = 8
VMEM_LIMIT = 56 * 1024 * 1024

FFN_ROWS = 512
FFN_COLS = 256
SEQ_ROWS = 512
PROJ_COLS = 256
ATT_Q = 512
ATT_K = 512
ATT_HEADS = 4
ATT_ROWS = 64
VT_ROWS = 144
LOG2E = 1.4426950408889634
STABILIZER_LIMIT = 50.0
SCORE_BOUND_MARGIN = 1.02


def _rms(xf, g, eps):
    ms = jnp.mean(xf * xf, axis=-1, keepdims=True)
    return xf * lax.rsqrt(ms + eps) * g


def _params(sem):
    return pltpu.CompilerParams(dimension_semantics=sem, vmem_limit_bytes=VMEM_LIMIT)


def _const_spec(shape):
    nd = len(shape)
    return pl.BlockSpec(shape, lambda *_: (0,) * nd, pipeline_mode=pl.Buffered(1))


def _layer_spec(shape, layer):
    nd = len(shape)
    return pl.BlockSpec((None,) + tuple(shape), lambda *_: (layer,) + (0,) * nd,
                        pipeline_mode=pl.Buffered(1))


def _ffn_kernel(*refs, with_attn):
    if with_attn:
        x_ref, ao_ref, wo_ref, g_ref, wg_ref, wu_ref, wd_ref, o_ref, act_ref = refs
        x = x_ref[...] + jnp.dot(ao_ref[...], wo_ref[...], preferred_element_type=F32)
    else:
        x_ref, g_ref, wg_ref, wu_ref, wd_ref, o_ref, act_ref = refs
        x = x_ref[...]
    h = _rms(x, g_ref[...], EPS).astype(BF16)
    for c in range(D_FF // FFN_COLS):
        sl = slice(c * FFN_COLS, (c + 1) * FFN_COLS)
        gate = jnp.dot(h, wg_ref[:, sl], preferred_element_type=F32)
        up = jnp.dot(h, wu_ref[:, sl], preferred_element_type=F32)
        act_ref[:, sl] = (gate * jax.nn.sigmoid(gate) * up).astype(BF16)
    y = jnp.dot(act_ref[...], wd_ref[...], preferred_element_type=F32)
    o_ref[...] = x + 0.5 * y


def _ffn(x2d, layer, g, wg, wu, wd, attn=None):
    m = x2d.shape[0]
    row_spec = pl.BlockSpec((FFN_ROWS, D_MODEL), lambda i: (i, 0))
    operands, in_specs = [x2d], [row_spec]
    if attn is not None:
        o2d, wo, j = attn
        operands += [o2d, wo]
        in_specs += [row_spec, _layer_spec((D_MODEL, D_MODEL), j)]
    operands += [g, wg, wu, wd]
    in_specs += [_layer_spec((1, D_MODEL), layer), _layer_spec((D_MODEL, D_FF), layer),
                 _layer_spec((D_MODEL, D_FF), layer), _layer_spec((D_FF, D_MODEL), layer)]
    return pl.pallas_call(
        functools.partial(_ffn_kernel, with_attn=attn is not None),
        out_shape=jax.ShapeDtypeStruct((m, D_MODEL), F32),
        grid=(m // FFN_ROWS,),
        in_specs=in_specs,
        out_specs=row_spec,
        scratch_shapes=[pltpu.VMEM((FFN_ROWS, D_FF), BF16)],
        compiler_params=_params(("arbitrary",)),
        name="ffn_attn_out" if attn is not None else "ffn",
    )(*operands)


def _rec_kernel(x_ref, g_ref, win_ref, cw_ref, cb_ref, wax_ref, ba_ref, bx_ref, lam_ref,
                wout_ref, o_ref, ext_ref, a_ref, b_ref, carry_ref):
    ts = SEQ_ROWS

    @pl.when(pl.program_id(1) == 0)
    def _():
        ext_ref[0:SUBLANES, :] = jnp.zeros((SUBLANES, D_RNN), F32)
        carry_ref[...] = jnp.zeros((SUBLANES, D_RNN), F32)

    x = x_ref[0]
    h = _rms(x, g_ref[...], EPS).astype(BF16)
    proj = jnp.dot(h, win_ref[...], preferred_element_type=F32)
    gate = proj[:, :D_RNN]

    ext_ref[SUBLANES:SUBLANES + ts, :] = proj[:, D_RNN:]
    conv = cb_ref[...]
    for k in range(CONV_WIDTH):
        start = SUBLANES - (CONV_WIDTH - 1) + k
        conv = conv + ext_ref[start:start + ts, :] * cw_ref[k:k + 1, :]
    ext_ref[0:SUBLANES, :] = ext_ref[ts:ts + SUBLANES, :]

    lam = lam_ref[...]
    neg = -lam
    softplus = jnp.maximum(neg, 0.0) + jnp.log1p(jnp.exp(-jnp.abs(neg)))
    conv_bf = conv.astype(BF16)
    for n in range(N_LRU_BLOCKS):
        sl = slice(n * LRU_BLOCK, (n + 1) * LRU_BLOCK)
        gax = jnp.dot(conv_bf[:, sl], wax_ref[n], preferred_element_type=F32)
        r = jax.nn.sigmoid(gax[:, :LRU_BLOCK] + ba_ref[:, sl])
        i = jax.nn.sigmoid(gax[:, LRU_BLOCK:] + bx_ref[:, sl])
        log_a = -LRU_C * r * softplus[:, sl]
        a = jnp.exp(log_a)
        a_ref[:, sl] = a
        one_minus_a2 = -jnp.tanh(log_a) * (a * a + 1.0)
        b_ref[:, sl] = jnp.sqrt(one_minus_a2) * (i * conv[:, sl])

    row = lax.broadcasted_iota(jnp.int32, (SUBLANES, D_RNN), 0)

    def group(gi, carry):
        off = pl.multiple_of(gi * SUBLANES, SUBLANES)
        a = a_ref[pl.ds(off, SUBLANES), :]
        b = b_ref[pl.ds(off, SUBLANES), :]
        for sh in (1, 2, 4):
            keep = row >= sh
            a_prev = jnp.where(keep, pltpu.roll(a, sh, 0), 1.0)
            b_prev = jnp.where(keep, pltpu.roll(b, sh, 0), 0.0)
            b = a * b_prev + b
            a = a * a_prev
        hh = a * carry + b
        b_ref[pl.ds(off, SUBLANES), :] = hh
        return jnp.broadcast_to(hh[SUBLANES - 1:SUBLANES, :], (SUBLANES, D_RNN))

    carry_ref[...] = lax.fori_loop(0, ts // SUBLANES, group, carry_ref[...], unroll=2)

    y = (jax.nn.gelu(gate, approximate=True) * b_ref[...]).astype(BF16)
    o_ref[0] = x + jnp.dot(y, wout_ref[...], preferred_element_type=F32)


def _rec(x3d, layer, a, g, win, cw, cb, wax, ba, bx, lam, wout):
    b, s, _ = x3d.shape
    seq_spec = pl.BlockSpec((1, SEQ_ROWS, D_MODEL), lambda bi, si: (bi, si, 0))
    vec = _layer_spec((1, D_RNN), a)
    return pl.pallas_call(
        _rec_kernel,
        out_shape=jax.ShapeDtypeStruct((b, s, D_MODEL), F32),
        grid=(b, s // SEQ_ROWS),
        in_specs=[seq_spec, _layer_spec((1, D_MODEL), layer),
                  _layer_spec((D_MODEL, 2 * D_RNN), a),
                  _layer_spec((CONV_WIDTH, D_RNN), a), vec,
                  _layer_spec((N_LRU_BLOCKS, LRU_BLOCK, 2 * LRU_BLOCK), a), vec, vec, vec,
                  _layer_spec((D_RNN, D_MODEL), a)],
        out_specs=seq_spec,
        scratch_shapes=[pltpu.VMEM((SEQ_ROWS + SUBLANES, D_RNN), F32),
                        pltpu.VMEM((SEQ_ROWS, D_RNN), F32),
                        pltpu.VMEM((SEQ_ROWS, D_RNN), F32),
                        pltpu.VMEM((SUBLANES, D_RNN), F32)],
        compiler_params=_params(("arbitrary", "arbitrary")),
        name="rec_block",
    )(x3d, g, win, cw, cb, wax, ba, bx, lam, wout)


def _head_norm_rope(t, hn, cos, sin_signed, first_half, ones_blk):
    sq = t * t
    hi = sq.astype(BF16)
    lo = (sq - hi.astype(F32)).astype(BF16)
    ssum = jnp.dot(jnp.concatenate([hi, lo], axis=1), ones_blk, preferred_element_type=F32)
    tn = t * lax.rsqrt(ssum * (1.0 / HEAD_DIM) + EPS) * hn
    partner = jnp.where(first_half, pltpu.roll(tn, LANES - HEAD_DIM // 2, 1),
                        pltpu.roll(tn, HEAD_DIM // 2, 1))
    return tn * cos + partner * sin_signed


def _rope_consts(rows):
    lane = lax.broadcasted_iota(jnp.int32, (rows, LANES), 1)
    first_half = (lane % HEAD_DIM) < (HEAD_DIM // 2)
    r = (lax.broadcasted_iota(jnp.int32, (2 * LANES, LANES), 0) % LANES) // HEAD_DIM
    c = lax.broadcasted_iota(jnp.int32, (2 * LANES, LANES), 1) // HEAD_DIM
    ones_blk = jnp.where(r == c, 1.0, 0.0).astype(BF16)
    return first_half, ones_blk


def _kv_kernel(x_ref, g_ref, wk_ref, wv_ref, hn_ref, cos_ref, sin_ref, k_ref, vt_ref):
    x = x_ref[0]
    h = _rms(x, g_ref[...], EPS).astype(BF16)
    first_half, ones_blk = _rope_consts(SEQ_ROWS)
    cos = cos_ref[...]
    sin_signed = sin_ref[...]
    hn = hn_ref[...]
    for cc in range(D_MODEL // PROJ_COLS):
        cols = slice(cc * PROJ_COLS, (cc + 1) * PROJ_COLS)
        k = jnp.dot(h, wk_ref[:, cols], preferred_element_type=F32)
        v = jnp.dot(h, wv_ref[:, cols], preferred_element_type=F32)
        for i in range(PROJ_COLS // LANES):
            c = cc * (PROJ_COLS // LANES) + i
            sl = slice(i * LANES, (i + 1) * LANES)
            kr = _head_norm_rope(k[:, sl], hn, cos, sin_signed, first_half, ones_blk)
            k_ref[0, c] = kr.astype(BF16)
            vt_ref[0, c, 0, :LANES, :] = v[:, sl].T.astype(BF16)
            pad_row = lax.broadcasted_iota(jnp.int32, (VT_ROWS - LANES, SEQ_ROWS), 0)
            vt_ref[0, c, 0, LANES:, :] = jnp.where(pad_row == 0, 1.0, 0.0).astype(BF16)


def _kv(x3d, g, wk, wv, hn, cos, sin_signed):
    b, s, _ = x3d.shape
    nt = s // SEQ_ROWS
    seq_spec = pl.BlockSpec((1, SEQ_ROWS, D_MODEL), lambda bi, si: (bi, si, 0))
    tab = pl.BlockSpec((SEQ_ROWS, LANES), lambda bi, si: (si, 0))
    return pl.pallas_call(
        _kv_kernel,
        out_shape=(jax.ShapeDtypeStruct((b, N_HEADS, s, LANES), BF16),
                   jax.ShapeDtypeStruct((b, N_HEADS, nt, VT_ROWS, SEQ_ROWS), BF16)),
        grid=(b, nt),
        in_specs=[seq_spec, _const_spec((1, D_MODEL)), _const_spec((D_MODEL, D_MODEL)),
                  _const_spec((D_MODEL, D_MODEL)), _const_spec((1, LANES)), tab, tab],
        out_specs=(pl.BlockSpec((1, N_HEADS, SEQ_ROWS, LANES), lambda bi, si: (bi, 0, si, 0)),
                   pl.BlockSpec((1, N_HEADS, 1, VT_ROWS, SEQ_ROWS),
                                lambda bi, si: (bi, 0, si, 0, 0))),
        compiler_params=_params(("arbitrary", "arbitrary")),
        name="kv_proj",
    )(x3d, g, wk, wv, hn, cos, sin_signed)


def _q_kernel(x_ref, g_ref, wq_ref, hn_ref, cos_ref, sin_ref, qt_ref):
    x = x_ref[0]
    h = _rms(x, g_ref[...], EPS).astype(BF16)
    first_half, ones_blk = _rope_consts(SEQ_ROWS)
    cos = cos_ref[...]
    sin_signed = sin_ref[...]
    hn = hn_ref[...]

    def project(cc):
        return jnp.dot(h, wq_ref[:, cc * PROJ_COLS:(cc + 1) * PROJ_COLS],
                       preferred_element_type=F32)

    n_chunks = D_MODEL // PROJ_COLS
    q_next = project(0)
    for cc in range(n_chunks):
        q = q_next
        if cc + 1 < n_chunks:
            q_next = project(cc + 1)
        for i in range(PROJ_COLS // LANES):
            c = cc * (PROJ_COLS // LANES) + i
            qr = _head_norm_rope(q[:, i * LANES:(i + 1) * LANES], hn, cos, sin_signed, first_half,
                                 ones_blk)
            qt_ref[0, c, 0] = (qr * (HEAD_DIM ** -0.5 * LOG2E)).T.astype(BF16)


def _qproj(x3d, layer, j, g, wq, hn, cos, sin_signed):
    b, s, _ = x3d.shape
    nt = s // SEQ_ROWS
    seq_spec = pl.BlockSpec((1, SEQ_ROWS, D_MODEL), lambda bi, si: (bi, si, 0))
    tab = pl.BlockSpec((SEQ_ROWS, LANES), lambda bi, si: (si, 0))
    return pl.pallas_call(
        _q_kernel,
        out_shape=jax.ShapeDtypeStruct((b, N_HEADS, nt, LANES, SEQ_ROWS), BF16),
        grid=(b, nt),
        in_specs=[seq_spec, _layer_spec((1, D_MODEL), layer), _layer_spec((D_MODEL, D_MODEL), j),
                  _layer_spec((1, LANES), j), tab, tab],
        out_specs=pl.BlockSpec((1, N_HEADS, 1, LANES, SEQ_ROWS), lambda bi, si: (bi, 0, si, 0, 0)),
        compiler_params=_params(("arbitrary", "arbitrary")),
        name="q_proj",
    )(x3d, g, wq, hn, cos, sin_signed)


def _attn_kernel(bound_ref, qt_ref, k_ref, vt_ref, lq1_ref, lk1_ref, lq2_ref, lk2_ref, sn_ref,
                 o_ref, s_ref, p_ref, m_ref, acc_ref, *, lambda_init):
    qi = pl.program_id(2)
    bound = bound_ref[0]
    chains = []
    for hh in range(ATT_HEADS):
        qt = qt_ref[0, hh, 0]
        zero = jnp.zeros((HEAD_DIM, ATT_Q), BF16)
        chains += [(hh, jnp.concatenate([qt[:HEAD_DIM], zero], axis=0)),
                   (hh, jnp.concatenate([zero, qt[HEAD_DIM:]], axis=0))]
    n_chain = len(chains)
    n_lane_groups = ATT_Q // LANES

    def scores(c, j):
        hh, qz = chains[c]
        kj = k_ref[0, hh, pl.ds(pl.multiple_of(j * ATT_K, ATT_K), ATT_K), :]
        s_ref[c] = jnp.dot(kj, qz, preferred_element_type=F32)

    def exp_blocks(c, shift, diagonal):
        for rb in range(ATT_K // ATT_ROWS):
            rows = slice(rb * ATT_ROWS, (rb + 1) * ATT_ROWS)
            first_col = ((rb * ATT_ROWS) // CHUNK) * CHUNK if diagonal else 0
            for g in range(n_lane_groups):
                cols = slice(g * LANES, (g + 1) * LANES)
                if (g + 1) * LANES <= first_col:
                    p_ref[c, rows, cols] = jnp.zeros((ATT_ROWS, LANES), BF16)
                    continue
                if jnp.ndim(shift) == 0:
                    sb = shift
                else:
                    sb = jnp.tile(shift[:, cols], (ATT_ROWS // SUBLANES, 1))
                p = jnp.exp2(s_ref[c, rows, cols] - sb)
                if g * LANES < first_col:
                    lane = lax.broadcasted_iota(jnp.int32, (ATT_ROWS, LANES), 1)
                    p = jnp.where(lane >= first_col - g * LANES, p, 0.0)
                p_ref[c, rows, cols] = p.astype(BF16)

    def pv(c, j):
        return jnp.dot(vt_ref[0, chains[c][0], j], p_ref[c], preferred_element_type=F32)

    def bounded_tile(j, diagonal):
        scores(0, j)
        scores(1, j)
        for c in range(n_chain):
            exp_blocks(c, bound, diagonal)
            if c + 2 < n_chain:
                scores(c + 2, j)
            acc_ref[c] += pv(c, j)

    def online_tile(j, diagonal):
        scores(0, j)
        scores(1, j)
        for c in range(n_chain):
            if diagonal:
                k_chunk = lax.broadcasted_iota(jnp.int32, (ATT_K, ATT_Q), 0) // CHUNK
                q_chunk = lax.broadcasted_iota(jnp.int32, (ATT_K, ATT_Q), 1) // CHUNK
                s_all = jnp.where(k_chunk <= q_chunk, s_ref[c], -jnp.inf)
            else:
                s_all = s_ref[c]
            m_old = m_ref[c]
            m_new = jnp.maximum(m_old, jnp.max(s_all, axis=0, keepdims=True))
            alpha = jnp.exp2(m_old - m_new)
            m_ref[c] = m_new
            exp_blocks(c, m_new, diagonal)
            if c + 2 < n_chain:
                scores(c + 2, j)
            acc_ref[c] = jnp.tile(alpha, (VT_ROWS // SUBLANES, 1)) * acc_ref[c] + pv(c, j)

    def attend(tile):
        acc_ref[...] = jnp.zeros(acc_ref.shape, F32)

        def full_tile(j, carry):
            tile(j, False)
            return carry

        lax.fori_loop(0, qi, full_tile, 0)
        tile(qi, True)

        lam = (jnp.exp(jnp.sum(lq1_ref[...] * lk1_ref[...], axis=-1, keepdims=True))
               - jnp.exp(jnp.sum(lq2_ref[...] * lk2_ref[...], axis=-1, keepdims=True))
               + lambda_init)
        gain = jnp.tile(sn_ref[...], (1, n_lane_groups))
        for hh in range(ATT_HEADS):
            c1, c2 = 2 * hh, 2 * hh + 1
            inv_l1 = 1.0 / acc_ref[c1, LANES:LANES + 1, :]
            inv_l2 = 1.0 / acc_ref[c2, LANES:LANES + 1, :]
            ot = acc_ref[c1, :LANES, :] * inv_l1 - lam * (acc_ref[c2, :LANES, :] * inv_l2)
            ms = jnp.mean(ot * ot, axis=0, keepdims=True)
            yt = ot * lax.rsqrt(ms + SUBLN_EPS) * gain * (1.0 - lambda_init)
            o_ref[0, :, hh * LANES:(hh + 1) * LANES] = yt.T.astype(BF16)

    @pl.when(bound <= STABILIZER_LIMIT)
    def _():
        attend(bounded_tile)

    @pl.when(jnp.logical_not(bound <= STABILIZER_LIMIT))
    def _():
        m_ref[...] = jnp.full(m_ref.shape, -jnp.inf, F32)
        attend(online_tile)


def _attn(qt, k, vt, j, bound, lq1, lk1, lq2, lk2, sn, lambda_init):
    b, nh, s, _ = k.shape
    nt = s // ATT_K
    vec = _const_spec((1, HEAD_DIM))
    lvec = _layer_spec((1, HEAD_DIM), j)
    n_chain = 2 * ATT_HEADS
    return pl.pallas_call(
        functools.partial(_attn_kernel, lambda_init=lambda_init),
        out_shape=jax.ShapeDtypeStruct((b, s, D_MODEL), BF16),
        grid=(b, nh // ATT_HEADS, s // ATT_Q),
        in_specs=[pl.BlockSpec(memory_space=pltpu.SMEM),
                  pl.BlockSpec((1, ATT_HEADS, 1, LANES, ATT_Q),
                               lambda bi, hi, qi: (bi, hi, qi, 0, 0)),
                  pl.BlockSpec((1, ATT_HEADS, s, LANES), lambda bi, hi, qi: (bi, hi, 0, 0)),
                  pl.BlockSpec((1, ATT_HEADS, nt, VT_ROWS, ATT_K),
                               lambda bi, hi, qi: (bi, hi, 0, 0, 0)),
                  lvec, vec, lvec, vec, _layer_spec((LANES, LANES), j)],
        out_specs=pl.BlockSpec((1, ATT_Q, ATT_HEADS * LANES), lambda bi, hi, qi: (bi, qi, hi)),
        scratch_shapes=[pltpu.VMEM((n_chain, ATT_K, ATT_Q), F32),
                        pltpu.VMEM((n_chain, ATT_K, ATT_Q), BF16),
                        pltpu.VMEM((n_chain, SUBLANES, ATT_Q), F32),
                        pltpu.VMEM((n_chain, VT_ROWS, ATT_Q), F32)],
        compiler_params=_params(("arbitrary", "arbitrary", "arbitrary")),
        name="diff_attn",
    )(bound, qt, k, vt, lq1, lk1, lq2, lk2, sn)


def _rope_tables(s):
    pos = jnp.arange(s, dtype=F32)
    inv_freq = ROPE_THETA ** (-jnp.arange(0, HEAD_DIM, 2, dtype=F32) / HEAD_DIM)
    ang = pos[:, None] * inv_freq[None, :]
    ang = jnp.concatenate([ang, ang], axis=-1)
    cos, sin = jnp.cos(ang), jnp.sin(ang)
    half = HEAD_DIM // 2
    sin_signed = jnp.concatenate([-sin[:, :half], sin[:, half:]], axis=-1)
    reps = LANES // HEAD_DIM
    return jnp.tile(cos, (1, reps)), jnp.tile(sin_signed, (1, reps))


def kernel(x, ffn1_norm, ffn1_w_gate, ffn1_w_up, ffn1_w_down, ffn2_norm, ffn2_w_gate, ffn2_w_up, ffn2_w_down, mix_norm, rec_w_in, rec_conv_w, rec_conv_b, rec_w_a, rec_b_a, rec_w_x, rec_b_x, rec_lambda, rec_w_out, kv_norm, w_k, w_v, k_norm, lambda_k1, lambda_k2, attn_w_q, q_norm, lambda_q1, lambda_q2, sub_norm, attn_w_o):
    b, s, d = x.shape
    m = b * s
    reps = LANES // HEAD_DIM
    cos, sin_signed = _rope_tables(s)

    f1 = (ffn1_w_gate.astype(BF16), ffn1_w_up.astype(BF16), ffn1_w_down.astype(BF16))
    f2 = (ffn2_w_gate.astype(BF16), ffn2_w_up.astype(BF16), ffn2_w_down.astype(BF16))
    w_in = rec_w_in.astype(BF16)
    w_ax = jnp.concatenate([rec_w_a, rec_w_x], axis=-1).astype(BF16)
    w_out = rec_w_out.astype(BF16)
    wk, wv = w_k.astype(BF16), w_v.astype(BF16)
    wq, wo = attn_w_q.astype(BF16), attn_w_o.astype(BF16)

    def row(vec):
        return vec.reshape(1, -1)

    def rows(stacked):
        return stacked.reshape(stacked.shape[0], 1, stacked.shape[1])

    g1, g2, gmix = rows(ffn1_norm), rows(ffn2_norm), rows(mix_norm)
    conv_b, b_a, b_x, lam = rows(rec_conv_b), rows(rec_b_a), rows(rec_b_x), rows(rec_lambda)
    q_hn = rows(jnp.tile(q_norm, (1, reps)))
    lq1, lq2 = rows(lambda_q1), rows(lambda_q2)
    sub_g = jnp.broadcast_to(sub_norm[:, :, None], sub_norm.shape + (LANES,))

    k_shared = vt_shared = None
    x2 = x.reshape(m, d)
    for layer in range(DEPTH):
        if layer == N_A_LAYERS:
            k_shared, vt_shared = _kv(x2.reshape(b, s, d), row(kv_norm), wk, wv,
                                      row(jnp.tile(k_norm, reps)), cos, sin_signed)
        x2 = _ffn(x2, layer, g1, *f1)
        if layer < N_A_LAYERS:
            a = layer
            x2 = _rec(x2.reshape(b, s, d), layer, a, gmix, w_in, rec_conv_w, conv_b, w_ax, b_a,
                      b_x, lam, w_out).reshape(m, d)
            x2 = _ffn(x2, layer, g2, *f2)
        else:
            j = layer - N_A_LAYERS
            lambda_init = 0.8 - 0.6 * math.exp(-0.3 * layer)
            qt = _qproj(x2.reshape(b, s, d), layer, j, gmix, wq, q_hn, cos, sin_signed)
            bound = (SCORE_BOUND_MARGIN * HEAD_DIM ** 0.5 * LOG2E
                     * jnp.max(jnp.abs(q_norm[j])) * jnp.max(jnp.abs(k_norm))).reshape(1)
            o = _attn(qt, k_shared, vt_shared, j, bound, lq1, row(lambda_k1), lq2, row(lambda_k2),
                      sub_g, lambda_init)
            x2 = _ffn(x2, layer, g2, *f2, attn=(o.reshape(m, d), wo, j))
    return x2.reshape(b, s, d)
```

```python
import functools
import math

import jax
import jax.numpy as jnp
from jax import lax
from jax.experimental import pallas as pl
from jax.experimental.pallas import tpu as pltpu

F32 = jnp.float32
BF16 = jnp.bfloat16

D_MODEL = 1024
D_FF = 2816
DEPTH = 4
N_A_LAYERS = DEPTH // 2
CHUNK = 64
D_RNN = D_MODEL
N_LRU_BLOCKS = 8
LRU_BLOCK = D_RNN // N_LRU_BLOCKS
CONV_WIDTH = 4
LRU_C = 8.0
N_HEADS = 8
HEAD_DIM = 64
ROPE_THETA = 10000.0
EPS = 1e-6
SUBLN_EPS = 1e-5

LANES = 128
SUBLANES = 8
VMEM_LIMIT = 56 * 1024 * 1024

FFN_ROWS = 512
FFN_COLS = 256
SEQ_ROWS = 512
REC_SEGS = SUBLANES
REC_SEG_LEN = SEQ_ROWS // REC_SEGS
REC_HALO = CONV_WIDTH - 1
REC_SLABS = D_RNN // LANES
PROJ_COLS = 256
ATT_Q = 512
ATT_K = 512
ATT_HEADS = 4
ATT_ROWS = 64
VT_ROWS = 144
LOG2E = 1.4426950408889634
STABILIZER_LIMIT = 50.0
SCORE_BOUND_MARGIN = 1.02


def _rms(xf, g, eps):
    ms = jnp.mean(xf * xf, axis=-1, keepdims=True)
    return xf * lax.rsqrt(ms + eps) * g


def _params(sem):
    return pltpu.CompilerParams(dimension_semantics=sem, vmem_limit_bytes=VMEM_LIMIT)


def _const_spec(shape):
    nd = len(shape)
    return pl.BlockSpec(shape, lambda *_: (0,) * nd, pipeline_mode=pl.Buffered(1))


def _layer_spec(shape, layer):
    nd = len(shape)
    return pl.BlockSpec((None,) + tuple(shape), lambda *_: (layer,) + (0,) * nd,
                        pipeline_mode=pl.Buffered(1))


def _ffn_kernel(*refs, layer, with_attn):
    if with_attn:
        x_ref, ao_ref, wo_ref, g_ref, wg_hbm, wu_hbm, wd_hbm, o_ref = refs[:8]
    else:
        x_ref, g_ref, wg_hbm, wu_hbm, wd_hbm, o_ref = refs[:6]
    act_ref, wg_ref, wu_ref, wd_ref, stage_g, stage_u, stage_d, sems = refs[-8:]
    step = pl.program_id(0)
    n_chunks = D_FF // FFN_COLS

    def chunk_copies(c, slot):
        cols = pl.ds(c * FFN_COLS, FFN_COLS)
        return (pltpu.make_async_copy(wg_hbm.at[layer, :, cols], stage_g.at[slot], sems.at[slot, 0]),
                pltpu.make_async_copy(wu_hbm.at[layer, :, cols], stage_u.at[slot], sems.at[slot, 1]),
                pltpu.make_async_copy(wd_hbm.at[layer, cols, :], stage_d.at[slot], sems.at[slot, 2]))

    def fetch_and_cast(c):
        slot = c % 2
        sl = slice(c * FFN_COLS, (c + 1) * FFN_COLS)
        if c + 1 < n_chunks:
            for cp in chunk_copies(c + 1, 1 - slot):
                cp.start()
        for cp in chunk_copies(c, slot):
            cp.wait()
        wg_ref[:, sl] = stage_g[slot].astype(BF16)
        wu_ref[:, sl] = stage_u[slot].astype(BF16)
        wd_ref[sl, :] = stage_d[slot].astype(BF16)

    def body(load_weights):
        if load_weights:
            for cp in chunk_copies(0, 0):
                cp.start()
        x = x_ref[...]
        if with_attn:
            x = x + jnp.dot(ao_ref[...], wo_ref[...], preferred_element_type=F32)
        h = _rms(x, g_ref[...], EPS).astype(BF16)
        for c in range(n_chunks):
            if load_weights:
                fetch_and_cast(c)
            sl = slice(c * FFN_COLS, (c + 1) * FFN_COLS)
            gate = jnp.dot(h, wg_ref[:, sl], preferred_element_type=F32)
            up = jnp.dot(h, wu_ref[:, sl], preferred_element_type=F32)
            act_ref[:, sl] = (gate * jax.nn.sigmoid(gate) * up).astype(BF16)
        y = jnp.dot(act_ref[...], wd_ref[...], preferred_element_type=F32)
        o_ref[...] = x + 0.5 * y

    @pl.when(step == 0)
    def _():
        body(True)

    @pl.when(step != 0)
    def _():
        body(False)


def _ffn(x2d, layer, g, wg, wu, wd, attn=None):
    m = x2d.shape[0]
    row_spec = pl.BlockSpec((FFN_ROWS, D_MODEL), lambda i: (i, 0))
    hbm = pl.BlockSpec(memory_space=pl.ANY)
    operands, in_specs = [x2d], [row_spec]
    if attn is not None:
        o2d, wo, j = attn
        operands += [o2d, wo]
        in_specs += [row_spec, _layer_spec((D_MODEL, D_MODEL), j)]
    operands += [g, wg, wu, wd]
    in_specs += [_layer_spec((1, D_MODEL), layer), hbm, hbm, hbm]
    return pl.pallas_call(
        functools.partial(_ffn_kernel, layer=layer, with_attn=attn is not None),
        out_shape=jax.ShapeDtypeStruct((m, D_MODEL), F32),
        grid=(m // FFN_ROWS,),
        in_specs=in_specs,
        out_specs=row_spec,
        scratch_shapes=[pltpu.VMEM((FFN_ROWS, D_FF), BF16),
                        pltpu.VMEM((D_MODEL, D_FF), BF16),
                        pltpu.VMEM((D_MODEL, D_FF), BF16),
                        pltpu.VMEM((D_FF, D_MODEL), BF16),
                        pltpu.VMEM((2, D_MODEL, FFN_COLS), F32),
                        pltpu.VMEM((2, D_MODEL, FFN_COLS), F32),
                        pltpu.VMEM((2, FFN_COLS, D_MODEL), F32),
                        pltpu.SemaphoreType.DMA((2, 3))],
        compiler_params=_params(("arbitrary",)),
        name="ffn_attn_out" if attn is not None else "ffn",
    )(*operands)


def _rec_kernel(x_ref, g_ref, win_ref, cw_ref, cb_ref, wax_ref, ba_ref, bx_ref, lam_ref,
                wout_ref, o_ref, ext_ref, a_ref, b_ref, tail_ref, carry_ref):
    ts = SEQ_ROWS
    halo_rows = REC_HALO * SUBLANES

    @pl.when(pl.program_id(1) == 0)
    def _():
        tail_ref[...] = jnp.zeros(tail_ref.shape, F32)
        carry_ref[...] = jnp.zeros(carry_ref.shape, F32)

    x = x_ref[0]
    h = _rms(x, g_ref[...], EPS).astype(BF16)
    proj = jnp.dot(h, win_ref[...], preferred_element_type=F32)
    gate = proj[:, :D_RNN]

    lam = lam_ref[...]
    neg = -lam
    softplus = jnp.maximum(neg, 0.0) + jnp.log1p(jnp.exp(-jnp.abs(neg)))
    half_c_softplus = (-0.5 * LRU_C) * softplus
    sub = lax.broadcasted_iota(jnp.int32, (SUBLANES, LANES), 0)

    for n in range(REC_SLABS):
        sl = slice(n * LANES, (n + 1) * LANES)
        for j in range(REC_SEGS):
            ext_ref[n, pl.ds(halo_rows + j, REC_SEG_LEN, stride=REC_SEGS), :] = (
                proj[j * REC_SEG_LEN:(j + 1) * REC_SEG_LEN, D_RNN + n * LANES:D_RNN + (n + 1) * LANES])
        for v in range(REC_HALO):
            rows = slice(SUBLANES * v, SUBLANES * (v + 1))
            cur = ext_ref[n, ts + SUBLANES * v:ts + SUBLANES * (v + 1), :]
            prev = tail_ref[n, rows, :]
            ext_ref[n, rows, :] = jnp.where(sub == 0, pltpu.roll(prev, 1, 0), pltpu.roll(cur, 1, 0))
            tail_ref[n, rows, :] = cur

        conv = cb_ref[:, sl]
        for k in range(CONV_WIDTH):
            conv = conv + ext_ref[n, SUBLANES * k:SUBLANES * k + ts, :] * cw_ref[k:k + 1, sl]

        gax = jnp.dot(conv.astype(BF16), wax_ref[n], preferred_element_type=F32)
        t_a = jnp.tanh(0.5 * (gax[:, :LRU_BLOCK] + ba_ref[:, sl]))
        i = 0.5 * jnp.tanh(0.5 * (gax[:, LRU_BLOCK:] + bx_ref[:, sl])) + 0.5
        log_a = half_c_softplus[:, sl] * t_a + half_c_softplus[:, sl]
        a = jnp.exp(log_a)
        a_ref[n] = a
        one_minus_a2 = -jnp.tanh(log_a) * (a * a + 1.0)
        b_ref[n] = jnp.sqrt(one_minus_a2) * (i * conv)

    hs = [jnp.zeros((SUBLANES, LANES), F32)] * REC_SLABS
    ps = [jnp.ones((SUBLANES, LANES), F32)] * REC_SLABS
    for i in range(REC_SEG_LEN):
        rows = slice(SUBLANES * i, SUBLANES * (i + 1))
        for n in range(REC_SLABS):
            a_i = a_ref[n, rows, :]
            hs[n] = a_i * hs[n] + b_ref[n, rows, :]
            ps[n] = a_i * ps[n]
            b_ref[n, rows, :] = hs[n]
            a_ref[n, rows, :] = ps[n]

    cols = []
    for n in range(REC_SLABS):
        carry_in = carry_ref[n]
        c = carry_in
        for _ in range(REC_SEGS - 1):
            c = jnp.where(sub == 0, carry_in, pltpu.roll(hs[n] + ps[n] * c, 1, 0))
        seg_end = hs[n] + ps[n] * c
        carry_ref[n] = jnp.broadcast_to(seg_end[SUBLANES - 1:SUBLANES, :], (SUBLANES, LANES))
        b_ref[n] = b_ref[n] + a_ref[n] * jnp.tile(c, (REC_SEG_LEN, 1))
        cols.append(jnp.concatenate(
            [b_ref[n, pl.ds(j, REC_SEG_LEN, stride=REC_SEGS), :] for j in range(REC_SEGS)], axis=0))
    h_scan = jnp.concatenate(cols, axis=1)

    y = (jax.nn.gelu(gate, approximate=True) * h_scan).astype(BF16)
    o_ref[0] = x + jnp.dot(y, wout_ref[...], preferred_element_type=F32)


def _rec(x3d, layer, a, g, win, cw, cb, wax, ba, bx, lam, wout):
    b, s, _ = x3d.shape
    seq_spec = pl.BlockSpec((1, SEQ_ROWS, D_MODEL), lambda bi, si: (bi, si, 0))
    vec = _layer_spec((1, D_RNN), a)
    return pl.pallas_call(
        _rec_kernel,
        out_shape=jax.ShapeDtypeStruct((b, s, D_MODEL), F32),
        grid=(b, s // SEQ_ROWS),
        in_specs=[seq_spec, _layer_spec((1, D_MODEL), layer),
                  _layer_spec((D_MODEL, 2 * D_RNN), a),
                  _layer_spec((CONV_WIDTH, D_RNN), a), vec,
                  _layer_spec((N_LRU_BLOCKS, LRU_BLOCK, 2 * LRU_BLOCK), a), vec, vec, vec,
                  _layer_spec((D_RNN, D_MODEL), a)],
        out_specs=seq_spec,
        scratch_shapes=[pltpu.VMEM((REC_SLABS, SEQ_ROWS + REC_HALO * SUBLANES, LANES), F32),
                        pltpu.VMEM((REC_SLABS, SEQ_ROWS, LANES), F32),
                        pltpu.VMEM((REC_SLABS, SEQ_ROWS, LANES), F32),
                        pltpu.VMEM((REC_SLABS, REC_HALO * SUBLANES, LANES), F32),
                        pltpu.VMEM((REC_SLABS, SUBLANES, LANES), F32)],
        compiler_params=_params(("arbitrary", "arbitrary")),
        name="rec_block",
    )(x3d, g, win, cw, cb, wax, ba, bx, lam, wout)


def _head_norm_rope(t, hn, cos, sin_signed, first_half, ones_blk):
    sq = t * t
    hi = sq.astype(BF16)
    lo = (sq - hi.astype(F32)).astype(BF16)
    ssum = jnp.dot(jnp.concatenate([hi, lo], axis=1), ones_blk, preferred_element_type=F32)
    tn = t * lax.rsqrt(ssum * (1.0 / HEAD_DIM) + EPS) * hn
    partner = jnp.where(first_half, pltpu.roll(tn, LANES - HEAD_DIM // 2, 1),
                        pltpu.roll(tn, HEAD_DIM // 2, 1))
    return tn * cos + partner * sin_signed


def _rope_consts(rows):
    lane = lax.broadcasted_iota(jnp.int32, (rows, LANES), 1)
    first_half = (lane % HEAD_DIM) < (HEAD_DIM // 2)
    r = (lax.broadcasted_iota(jnp.int32, (2 * LANES, LANES), 0) % LANES) // HEAD_DIM
    c = lax.broadcasted_iota(jnp.int32, (2 * LANES, LANES), 1) // HEAD_DIM
    ones_blk = jnp.where(r == c, 1.0, 0.0).astype(BF16)
    return first_half, ones_blk


def _kv_kernel(x_ref, g_ref, wk_ref, wv_ref, hn_ref, cos_ref, sin_ref, k_ref, vt_ref):
    x = x_ref[0]
    h = _rms(x, g_ref[...], EPS).astype(BF16)
    first_half, ones_blk = _rope_consts(SEQ_ROWS)
    cos = cos_ref[...]
    sin_signed = sin_ref[...]
    hn = hn_ref[...]
    for cc in range(D_MODEL // PROJ_COLS):
        cols = slice(cc * PROJ_COLS, (cc + 1) * PROJ_COLS)
        k = jnp.dot(h, wk_ref[:, cols], preferred_element_type=F32)
        v = jnp.dot(h, wv_ref[:, cols], preferred_element_type=F32)
        for i in range(PROJ_COLS // LANES):
            c = cc * (PROJ_COLS // LANES) + i
            sl = slice(i * LANES, (i + 1) * LANES)
            kr = _head_norm_rope(k[:, sl], hn, cos, sin_signed, first_half, ones_blk)
            k_ref[0, c] = kr.astype(BF16)
            vt_ref[0, c, 0, :LANES, :] = v[:, sl].T.astype(BF16)
            pad_row = lax.broadcasted_iota(jnp.int32, (VT_ROWS - LANES, SEQ_ROWS), 0)
            vt_ref[0, c, 0, LANES:, :] = jnp.where(pad_row == 0, 1.0, 0.0).astype(BF16)


def _kv(x3d, g, wk, wv, hn, cos, sin_signed):
    b, s, _ = x3d.shape
    nt = s // SEQ_ROWS
    seq_spec = pl.BlockSpec((1, SEQ_ROWS, D_MODEL), lambda bi, si: (bi, si, 0))
    tab = pl.BlockSpec((SEQ_ROWS, LANES), lambda bi, si: (si, 0))
    return pl.pallas_call(
        _kv_kernel,
        out_shape=(jax.ShapeDtypeStruct((b, N_HEADS, s, LANES), BF16),
                   jax.ShapeDtypeStruct((b, N_HEADS, nt, VT_ROWS, SEQ_ROWS), BF16)),
        grid=(b, nt),
        in_specs=[seq_spec, _const_spec((1, D_MODEL)), _const_spec((D_MODEL, D_MODEL)),
                  _const_spec((D_MODEL, D_MODEL)), _const_spec((1, LANES)), tab, tab],
        out_specs=(pl.BlockSpec((1, N_HEADS, SEQ_ROWS, LANES), lambda bi, si: (bi, 0, si, 0)),
                   pl.BlockSpec((1, N_HEADS, 1, VT_ROWS, SEQ_ROWS),
                                lambda bi, si: (bi, 0, si, 0, 0))),
        compiler_params=_params(("arbitrary", "arbitrary")),
        name="kv_proj",
    )(x3d, g, wk, wv, hn, cos, sin_signed)


def _q_kernel(x_ref, g_ref, wq_ref, hn_ref, cos_ref, sin_ref, qt_ref):
    x = x_ref[0]
    h = _rms(x, g_ref[...], EPS).astype(BF16)
    first_half, ones_blk = _rope_consts(SEQ_ROWS)
    cos = cos_ref[...]
    sin_signed = sin_ref[...]
    hn = hn_ref[...]

    def project(cc):
        return jnp.dot(h, wq_ref[:, cc * PROJ_COLS:(cc + 1) * PROJ_COLS],
                       preferred_element_type=F32)

    n_chunks = D_MODEL // PROJ_COLS
    q_next = project(0)
    for cc in range(n_chunks):
        q = q_next
        if cc + 1 < n_chunks:
            q_next = project(cc + 1)
        for i in range(PROJ_COLS // LANES):
            c = cc * (PROJ_COLS // LANES) + i
            qr = _head_norm_rope(q[:, i * LANES:(i + 1) * LANES], hn, cos, sin_signed, first_half,
                                 ones_blk)
            qt_ref[0, c, 0] = (qr * (HEAD_DIM ** -0.5 * LOG2E)).T.astype(BF16)


def _qproj(x3d, layer, j, g, wq, hn, cos, sin_signed):
    b, s, _ = x3d.shape
    nt = s // SEQ_ROWS
    seq_spec = pl.BlockSpec((1, SEQ_ROWS, D_MODEL), lambda bi, si: (bi, si, 0))
    tab = pl.BlockSpec((SEQ_ROWS, LANES), lambda bi, si: (si, 0))
    return pl.pallas_call(
        _q_kernel,
        out_shape=jax.ShapeDtypeStruct((b, N_HEADS, nt, LANES, SEQ_ROWS), BF16),
        grid=(b, nt),
        in_specs=[seq_spec, _layer_spec((1, D_MODEL), layer), _layer_spec((D_MODEL, D_MODEL), j),
                  _layer_spec((1, LANES), j), tab, tab],
        out_specs=pl.BlockSpec((1, N_HEADS, 1, LANES, SEQ_ROWS), lambda bi, si: (bi, 0, si, 0, 0)),
        compiler_params=_params(("arbitrary", "arbitrary")),
        name="q_proj",
    )(x3d, g, wq, hn, cos, sin_signed)


def _attn_kernel(bound_ref, qt_ref, k_ref, vt_ref, lq1_ref, lk1_ref, lq2_ref, lk2_ref, sn_ref,
                 o_ref, s_ref, p_ref, m_ref, acc_ref, *, lambda_init):
    qi = pl.program_id(2)
    bound = bound_ref[0]
    chains = []
    for hh in range(ATT_HEADS):
        qt = qt_ref[0, hh, 0]
        zero = jnp.zeros((HEAD_DIM, ATT_Q), BF16)
        chains += [(hh, jnp.concatenate([qt[:HEAD_DIM], zero], axis=0)),
                   (hh, jnp.concatenate([zero, qt[HEAD_DIM:]], axis=0))]
    n_chain = len(chains)
    n_lane_groups = ATT_Q // LANES

    def scores(c, j):
        hh, qz = chains[c]
        kj = k_ref[0, hh, pl.ds(pl.multiple_of(j * ATT_K, ATT_K), ATT_K), :]
        s_ref[c] = jnp.dot(kj, qz, preferred_element_type=F32)

    def exp_blocks(c, shift, diagonal):
        for rb in range(ATT_K // ATT_ROWS):
            rows = slice(rb * ATT_ROWS, (rb + 1) * ATT_ROWS)
            first_col = ((rb * ATT_ROWS) // CHUNK) * CHUNK if diagonal else 0
            for g in range(n_lane_groups):
                cols = slice(g * LANES, (g + 1) * LANES)
                if (g + 1) * LANES <= first_col:
                    p_ref[c, rows, cols] = jnp.zeros((ATT_ROWS, LANES), BF16)
                    continue
                if jnp.ndim(shift) == 0:
                    sb = shift
                else:
                    sb = jnp.tile(shift[:, cols], (ATT_ROWS // SUBLANES, 1))
                p = jnp.exp2(s_ref[c, rows, cols] - sb)
                if g * LANES < first_col:
                    lane = lax.broadcasted_iota(jnp.int32, (ATT_ROWS, LANES), 1)
                    p = jnp.where(lane >= first_col - g * LANES, p, 0.0)
                p_ref[c, rows, cols] = p.astype(BF16)

    def pv(c, j):
        return jnp.dot(vt_ref[0, chains[c][0], j], p_ref[c], preferred_element_type=F32)

    def bounded_tile(j, diagonal):
        scores(0, j)
        scores(1, j)
        for c in range(n_chain):
            exp_blocks(c, bound, diagonal)
            if c + 2 < n_chain:
                scores(c + 2, j)
            acc_ref[c] += pv(c, j)

    def online_tile(j, diagonal):
        scores(0, j)
        scores(1, j)
        for c in range(n_chain):
            if diagonal:
                k_chunk = lax.broadcasted_iota(jnp.int32, (ATT_K, ATT_Q), 0) // CHUNK
                q_chunk = lax.broadcasted_iota(jnp.int32, (ATT_K, ATT_Q), 1) // CHUNK
                s_all = jnp.where(k_chunk <= q_chunk, s_ref[c], -jnp.inf)
            else:
                s_all = s_ref[c]
            m_old = m_ref[c]
            m_new = jnp.maximum(m_old, jnp.max(s_all, axis=0, keepdims=True))
            alpha = jnp.exp2(m_old - m_new)
            m_ref[c] = m_new
            exp_blocks(c, m_new, diagonal)
            if c + 2 < n_chain:
                scores(c + 2, j)
            acc_ref[c] = jnp.tile(alpha, (VT_ROWS // SUBLANES, 1)) * acc_ref[c] + pv(c, j)

    def attend(tile):
        acc_ref[...] = jnp.zeros(acc_ref.shape, F32)

        def full_tile(j, carry):
            tile(j, False)
            return carry

        lax.fori_loop(0, qi, full_tile, 0)
        tile(qi, True)

        lam = (jnp.exp(jnp.sum(lq1_ref[...] * lk1_ref[...], axis=-1, keepdims=True))
               - jnp.exp(jnp.sum(lq2_ref[...] * lk2_ref[...], axis=-1, keepdims=True))
               + lambda_init)
        gain = jnp.tile(sn_ref[...], (1, n_lane_groups))
        for hh in range(ATT_HEADS):
            c1, c2 = 2 * hh, 2 * hh + 1
            inv_l1 = 1.0 / acc_ref[c1, LANES:LANES + 1, :]
            inv_l2 = 1.0 / acc_ref[c2, LANES:LANES + 1, :]
            ot = acc_ref[c1, :LANES, :] * inv_l1 - lam * (acc_ref[c2, :LANES, :] * inv_l2)
            ms = jnp.mean(ot * ot, axis=0, keepdims=True)
            yt = ot * lax.rsqrt(ms + SUBLN_EPS) * gain * (1.0 - lambda_init)
            o_ref[0, :, hh * LANES:(hh + 1) * LANES] = yt.T.astype(BF16)

    @pl.when(bound <= STABILIZER_LIMIT)
    def _():
        attend(bounded_tile)

    @pl.when(jnp.logical_not(bound <= STABILIZER_LIMIT))
    def _():
        m_ref[...] = jnp.full(m_ref.shape, -jnp.inf, F32)
        attend(online_tile)


def _attn(qt, k, vt, j, bound, lq1, lk1, lq2, lk2, sn, lambda_init):
    b, nh, s, _ = k.shape
    nt = s // ATT_K
    vec = _const_spec((1, HEAD_DIM))
    lvec = _layer_spec((1, HEAD_DIM), j)
    n_chain = 2 * ATT_HEADS
    return pl.pallas_call(
        functools.partial(_attn_kernel, lambda_init=lambda_init),
        out_shape=jax.ShapeDtypeStruct((b, s, D_MODEL), BF16),
        grid=(b, nh // ATT_HEADS, s // ATT_Q),
        in_specs=[pl.BlockSpec(memory_space=pltpu.SMEM),
                  pl.BlockSpec((1, ATT_HEADS, 1, LANES, ATT_Q),
                               lambda bi, hi, qi: (bi, hi, qi, 0, 0)),
                  pl.BlockSpec((1, ATT_HEADS, s, LANES), lambda bi, hi, qi: (bi, hi, 0, 0)),
                  pl.BlockSpec((1, ATT_HEADS, nt, VT_ROWS, ATT_K),
                               lambda bi, hi, qi: (bi, hi, 0, 0, 0)),
                  lvec, vec, lvec, vec, _layer_spec((LANES, LANES), j)],
        out_specs=pl.BlockSpec((1, ATT_Q, ATT_HEADS * LANES), lambda bi, hi, qi: (bi, qi, hi)),
        scratch_shapes=[pltpu.VMEM((n_chain, ATT_K, ATT_Q), F32),
                        pltpu.VMEM((n_chain, ATT_K, ATT_Q), BF16),
                        pltpu.VMEM((n_chain, SUBLANES, ATT_Q), F32),
                        pltpu.VMEM((n_chain, VT_ROWS, ATT_Q), F32)],
        compiler_params=_params(("arbitrary", "arbitrary", "arbitrary")),
        name="diff_attn",
    )(bound, qt, k, vt, lq1, lk1, lq2, lk2, sn)


def _rope_tables(s):
    pos = jnp.arange(s, dtype=F32)
    inv_freq = ROPE_THETA ** (-jnp.arange(0, HEAD_DIM, 2, dtype=F32) / HEAD_DIM)
    ang = pos[:, None] * inv_freq[None, :]
    ang = jnp.concatenate([ang, ang], axis=-1)
    cos, sin = jnp.cos(ang), jnp.sin(ang)
    half = HEAD_DIM // 2
    sin_signed = jnp.concatenate([-sin[:, :half], sin[:, half:]], axis=-1)
    reps = LANES // HEAD_DIM
    return jnp.tile(cos, (1, reps)), jnp.tile(sin_signed, (1, reps))


def kernel(x, ffn1_norm, ffn1_w_gate, ffn1_w_up, ffn1_w_down, ffn2_norm, ffn2_w_gate, ffn2_w_up, ffn2_w_down, mix_norm, rec_w_in, rec_conv_w, rec_conv_b, rec_w_a, rec_b_a, rec_w_x, rec_b_x, rec_lambda, rec_w_out, kv_norm, w_k, w_v, k_norm, lambda_k1, lambda_k2, attn_w_q, q_norm, lambda_q1, lambda_q2, sub_norm, attn_w_o):
    b, s, d = x.shape
    m = b * s
    reps = LANES // HEAD_DIM
    cos, sin_signed = _rope_tables(s)

    f1 = (ffn1_w_gate, ffn1_w_up, ffn1_w_down)
    f2 = (ffn2_w_gate, ffn2_w_up, ffn2_w_down)
    w_in = rec_w_in.astype(BF16)
    w_ax = jnp.concatenate([rec_w_a, rec_w_x], axis=-1).astype(BF16)
    w_out = rec_w_out.astype(BF16)
    wk, wv = w_k.astype(BF16), w_v.astype(BF16)
    wq, wo = attn_w_q.astype(BF16), attn_w_o.astype(BF16)

    def row(vec):
        return vec.reshape(1, -1)

    def rows(stacked):
        return stacked.reshape(stacked.shape[0], 1, stacked.shape[1])

    g1, g2, gmix = rows(ffn1_norm), rows(ffn2_norm), rows(mix_norm)
    conv_b, b_a, b_x, lam = rows(rec_conv_b), rows(rec_b_a), rows(rec_b_x), rows(rec_lambda)
    q_hn = rows(jnp.tile(q_norm, (1, reps)))
    lq1, lq2 = rows(lambda_q1), rows(lambda_q2)
    sub_g = jnp.broadcast_to(sub_norm[:, :, None], sub_norm.shape + (LANES,))

    k_shared = vt_shared = None
    x2 = x.reshape(m, d)
    for layer in range(DEPTH):
        if layer == N_A_LAYERS:
            k_shared, vt_shared = _kv(x2.reshape(b, s, d), row(kv_norm), wk, wv,
                                      row(jnp.tile(k_norm, reps)), cos, sin_signed)
        x2 = _ffn(x2, layer, g1, *f1)
        if layer < N_A_LAYERS:
            a = layer
            x2 = _rec(x2.reshape(b, s, d), layer, a, gmix, w_in, rec_conv_w, conv_b, w_ax, b_a,
                      b_x, lam, w_out).reshape(m, d)
            x2 = _ffn(x2, layer, g2, *f2)
        else:
            j = layer - N_A_LAYERS
            lambda_init = 0.8 - 0.6 * math.exp(-0.3 * layer)
            qt = _qproj(x2.reshape(b, s, d), layer, j, gmix, wq, q_hn, cos, sin_signed)
            bound = (SCORE_BOUND_MARGIN * HEAD_DIM ** 0.5 * LOG2E
                     * jnp.max(jnp.abs(q_norm[j])) * jnp.max(jnp.abs(k_norm))).reshape(1)
            o = _attn(qt, k_shared, vt_shared, j, bound, lq1, row(lambda_k1), lq2, row(lambda_k2),
                      sub_g, lambda_init)
            x2 = _ffn(x2, layer, g2, *f2, attn=(o.reshape(m, d), wo, j))
    return x2.reshape(b, s, d)
```

```python
import functools
import math

import jax
import jax.numpy as jnp
from jax import lax
from jax.experimental import pallas as pl
from jax.experimental.pallas import tpu as pltpu

F32 = jnp.float32
BF16 = jnp.bfloat16

D_MODEL = 1024
D_FF = 2816
DEPTH = 4
N_A_LAYERS = DEPTH // 2
CHUNK = 64
D_RNN = D_MODEL
N_LRU_BLOCKS = 8
LRU_BLOCK = D_RNN // N_LRU_BLOCKS
CONV_WIDTH = 4
LRU_C = 8.0
N_HEADS = 8
HEAD_DIM = 64
ROPE_THETA = 10000.0
EPS = 1e-6
SUBLN_EPS = 1e-5

LANES = 128
SUBLANES = 8
VMEM_LIMIT = 56 * 1024 * 1024

FFN_ROWS = 512
FFN_COLS = 256
FFN_HEAD_GROUPS = 2
SEQ_ROWS = 512
REC_SEGS = SUBLANES
REC_SEG_LEN = SEQ_ROWS // REC_SEGS
REC_HALO = CONV_WIDTH - 1
REC_SLABS = D_RNN // LANES
PROJ_COLS = 256
ATT_Q = 512
ATT_K = 512
ATT_HEADS = 4
ATT_ROWS = 64
VT_ROWS = 144
LOG2E = 1.4426950408889634
STABILIZER_LIMIT = 50.0
SCORE_BOUND_MARGIN = 1.02


def _rms(xf, g, eps):
    ms = jnp.mean(xf * xf, axis=-1, keepdims=True)
    return xf * lax.rsqrt(ms + eps) * g


def _params(sem):
    return pltpu.CompilerParams(dimension_semantics=sem, vmem_limit_bytes=VMEM_LIMIT)


def _const_spec(shape):
    nd = len(shape)
    return pl.BlockSpec(shape, lambda *_: (0,) * nd, pipeline_mode=pl.Buffered(1))


def _layer_spec(shape, layer):
    nd = len(shape)
    return pl.BlockSpec((None,) + tuple(shape), lambda *_: (layer,) + (0,) * nd,
                        pipeline_mode=pl.Buffered(1))


def _ffn_kernel(*refs, layer, with_attn):
    if with_attn:
        x_ref, ao_ref, wo_ref, g_ref, wg_hbm, wu_hbm, wd_hbm, o_ref = refs[:8]
    else:
        x_ref, g_ref, wg_hbm, wu_hbm, wd_hbm, o_ref = refs[:6]
    act_ref, wg_ref, wu_ref, wd_ref, stage_g, stage_u, stage_d, sems = refs[-8:]
    step = pl.program_id(0)
    n_chunks = D_FF // FFN_COLS

    def chunk_copies(c, slot):
        cols = pl.ds(c * FFN_COLS, FFN_COLS)
        return (pltpu.make_async_copy(wg_hbm.at[layer, :, cols], stage_g.at[slot], sems.at[slot, 0]),
                pltpu.make_async_copy(wu_hbm.at[layer, :, cols], stage_u.at[slot], sems.at[slot, 1]),
                pltpu.make_async_copy(wd_hbm.at[layer, cols, :], stage_d.at[slot], sems.at[slot, 2]))

    def fetch_and_cast(c):
        slot = c % 2
        sl = slice(c * FFN_COLS, (c + 1) * FFN_COLS)
        if c + 1 < n_chunks:
            for cp in chunk_copies(c + 1, 1 - slot):
                cp.start()
        for cp in chunk_copies(c, slot):
            cp.wait()
        wg_ref[:, sl] = stage_g[slot].astype(BF16)
        wu_ref[:, sl] = stage_u[slot].astype(BF16)
        wd_ref[sl, :] = stage_d[slot].astype(BF16)

    def body(load_weights):
        if load_weights:
            for cp in chunk_copies(0, 0):
                cp.start()
        group_rows = FFN_ROWS // FFN_HEAD_GROUPS
        xs, hs = [], []
        for r in range(FFN_HEAD_GROUPS):
            rows = slice(r * group_rows, (r + 1) * group_rows)
            xr = x_ref[rows, :]
            if with_attn:
                xr = xr + jnp.dot(ao_ref[rows, :], wo_ref[...], preferred_element_type=F32)
            xs.append(xr)
            hs.append(_rms(xr, g_ref[...], EPS).astype(BF16))
        if load_weights:
            fetch_and_cast(0)
        first = slice(0, FFN_COLS)
        for r in range(FFN_HEAD_GROUPS):
            rows = slice(r * group_rows, (r + 1) * group_rows)
            gate = jnp.dot(hs[r], wg_ref[:, first], preferred_element_type=F32)
            up = jnp.dot(hs[r], wu_ref[:, first], preferred_element_type=F32)
            act_ref[rows, first] = (gate * jax.nn.sigmoid(gate) * up).astype(BF16)
        x = jnp.concatenate(xs, axis=0)
        h = jnp.concatenate(hs, axis=0)
        for c in range(1, n_chunks):
            if load_weights:
                fetch_and_cast(c)
            sl = slice(c * FFN_COLS, (c + 1) * FFN_COLS)
            gate = jnp.dot(h, wg_ref[:, sl], preferred_element_type=F32)
            up = jnp.dot(h, wu_ref[:, sl], preferred_element_type=F32)
            act_ref[:, sl] = (gate * jax.nn.sigmoid(gate) * up).astype(BF16)
        y = jnp.dot(act_ref[...], wd_ref[...], preferred_element_type=F32)
        o_ref[...] = x + 0.5 * y

    @pl.when(step == 0)
    def _():
        body(True)

    @pl.when(step != 0)
    def _():
        body(False)


def _ffn(x2d, layer, g, wg, wu, wd, attn=None):
    m = x2d.shape[0]
    row_spec = pl.BlockSpec((FFN_ROWS, D_MODEL), lambda i: (i, 0))
    hbm = pl.BlockSpec(memory_space=pl.ANY)
    operands, in_specs = [x2d], [row_spec]
    if attn is not None:
        o2d, wo, j = attn
        operands += [o2d, wo]
        in_specs += [row_spec, _layer_spec((D_MODEL, D_MODEL), j)]
    operands += [g, wg, wu, wd]
    in_specs += [_layer_spec((1, D_MODEL), layer), hbm, hbm, hbm]
    return pl.pallas_call(
        functools.partial(_ffn_kernel, layer=layer, with_attn=attn is not None),
        out_shape=jax.ShapeDtypeStruct((m, D_MODEL), F32),
        grid=(m // FFN_ROWS,),
        in_specs=in_specs,
        out_specs=row_spec,
        scratch_shapes=[pltpu.VMEM((FFN_ROWS, D_FF), BF16),
                        pltpu.VMEM((D_MODEL, D_FF), BF16),
                        pltpu.VMEM((D_MODEL, D_FF), BF16),
                        pltpu.VMEM((D_FF, D_MODEL), BF16),
                        pltpu.VMEM((2, D_MODEL, FFN_COLS), F32),
                        pltpu.VMEM((2, D_MODEL, FFN_COLS), F32),
                        pltpu.VMEM((2, FFN_COLS, D_MODEL), F32),
                        pltpu.SemaphoreType.DMA((2, 3))],
        compiler_params=_params(("arbitrary",)),
        name="ffn_attn_out" if attn is not None else "ffn",
    )(*operands)


def _rec_kernel(x_ref, g_ref, win_ref, cw_ref, cb_ref, wax_ref, ba_ref, bx_ref, lam_ref,
                wout_ref, o_ref, ext_ref, a_ref, b_ref, tail_ref, carry_ref):
    ts = SEQ_ROWS
    halo_rows = REC_HALO * SUBLANES

    @pl.when(pl.program_id(1) == 0)
    def _():
        tail_ref[...] = jnp.zeros(tail_ref.shape, F32)
        carry_ref[...] = jnp.zeros(carry_ref.shape, F32)

    x = x_ref[0]
    h = _rms(x, g_ref[...], EPS).astype(BF16)
    proj = jnp.dot(h, win_ref[...], preferred_element_type=F32)
    gate = proj[:, :D_RNN]

    lam = lam_ref[...]
    neg = -lam
    softplus = jnp.maximum(neg, 0.0) + jnp.log1p(jnp.exp(-jnp.abs(neg)))
    half_c_softplus = (-0.5 * LRU_C) * softplus
    sub = lax.broadcasted_iota(jnp.int32, (SUBLANES, LANES), 0)

    for n in range(REC_SLABS):
        sl = slice(n * LANES, (n + 1) * LANES)
        for j in range(REC_SEGS):
            ext_ref[n, pl.ds(halo_rows + j, REC_SEG_LEN, stride=REC_SEGS), :] = (
                proj[j * REC_SEG_LEN:(j + 1) * REC_SEG_LEN, D_RNN + n * LANES:D_RNN + (n + 1) * LANES])
        for v in range(REC_HALO):
            rows = slice(SUBLANES * v, SUBLANES * (v + 1))
            cur = ext_ref[n, ts + SUBLANES * v:ts + SUBLANES * (v + 1), :]
            prev = tail_ref[n, rows, :]
            ext_ref[n, rows, :] = jnp.where(sub == 0, pltpu.roll(prev, 1, 0), pltpu.roll(cur, 1, 0))
            tail_ref[n, rows, :] = cur

        conv = cb_ref[:, sl]
        for k in range(CONV_WIDTH):
            conv = conv + ext_ref[n, SUBLANES * k:SUBLANES * k + ts, :] * cw_ref[k:k + 1, sl]

        gax = jnp.dot(conv.astype(BF16), wax_ref[n], preferred_element_type=F32)
        t_a = jnp.tanh(0.5 * (gax[:, :LRU_BLOCK] + ba_ref[:, sl]))
        i = 0.5 * jnp.tanh(0.5 * (gax[:, LRU_BLOCK:] + bx_ref[:, sl])) + 0.5
        log_a = half_c_softplus[:, sl] * t_a + half_c_softplus[:, sl]
        a = jnp.exp(log_a)
        a_ref[n] = a
        one_minus_a2 = -jnp.tanh(log_a) * (a * a + 1.0)
        b_ref[n] = jnp.sqrt(one_minus_a2) * (i * conv)

    hs = [jnp.zeros((SUBLANES, LANES), F32)] * REC_SLABS
    ps = [jnp.ones((SUBLANES, LANES), F32)] * REC_SLABS
    for i in range(REC_SEG_LEN):
        rows = slice(SUBLANES * i, SUBLANES * (i + 1))
        for n in range(REC_SLABS):
            a_i = a_ref[n, rows, :]
            hs[n] = a_i * hs[n] + b_ref[n, rows, :]
            ps[n] = a_i * ps[n]
            b_ref[n, rows, :] = hs[n]
            a_ref[n, rows, :] = ps[n]

    cols = []
    for n in range(REC_SLABS):
        carry_in = carry_ref[n]
        c = carry_in
        for _ in range(REC_SEGS - 1):
            c = jnp.where(sub == 0, carry_in, pltpu.roll(hs[n] + ps[n] * c, 1, 0))
        seg_end = hs[n] + ps[n] * c
        carry_ref[n] = jnp.broadcast_to(seg_end[SUBLANES - 1:SUBLANES, :], (SUBLANES, LANES))
        b_ref[n] = b_ref[n] + a_ref[n] * jnp.tile(c, (REC_SEG_LEN, 1))
        cols.append(jnp.concatenate(
            [b_ref[n, pl.ds(j, REC_SEG_LEN, stride=REC_SEGS), :] for j in range(REC_SEGS)], axis=0))
    h_scan = jnp.concatenate(cols, axis=1)

    y = (jax.nn.gelu(gate, approximate=True) * h_scan).astype(BF16)
    o_ref[0] = x + jnp.dot(y, wout_ref[...], preferred_element_type=F32)


def _rec(x3d, layer, a, g, win, cw, cb, wax, ba, bx, lam, wout):
    b, s, _ = x3d.shape
    seq_spec = pl.BlockSpec((1, SEQ_ROWS, D_MODEL), lambda bi, si: (bi, si, 0))
    vec = _layer_spec((1, D_RNN), a)
    return pl.pallas_call(
        _rec_kernel,
        out_shape=jax.ShapeDtypeStruct((b, s, D_MODEL), F32),
        grid=(b, s // SEQ_ROWS),
        in_specs=[seq_spec, _layer_spec((1, D_MODEL), layer),
                  _layer_spec((D_MODEL, 2 * D_RNN), a),
                  _layer_spec((CONV_WIDTH, D_RNN), a), vec,
                  _layer_spec((N_LRU_BLOCKS, LRU_BLOCK, 2 * LRU_BLOCK), a), vec, vec, vec,
                  _layer_spec((D_RNN, D_MODEL), a)],
        out_specs=seq_spec,
        scratch_shapes=[pltpu.VMEM((REC_SLABS, SEQ_ROWS + REC_HALO * SUBLANES, LANES), F32),
                        pltpu.VMEM((REC_SLABS, SEQ_ROWS, LANES), F32),
                        pltpu.VMEM((REC_SLABS, SEQ_ROWS, LANES), F32),
                        pltpu.VMEM((REC_SLABS, REC_HALO * SUBLANES, LANES), F32),
                        pltpu.VMEM((REC_SLABS, SUBLANES, LANES), F32)],
        compiler_params=_params(("arbitrary", "arbitrary")),
        name="rec_block",
    )(x3d, g, win, cw, cb, wax, ba, bx, lam, wout)


def _head_norm_rope(t, hn, cos, sin_signed, first_half, ones_blk):
    sq = t * t
    hi = sq.astype(BF16)
    lo = (sq - hi.astype(F32)).astype(BF16)
    ssum = jnp.dot(jnp.concatenate([hi, lo], axis=1), ones_blk, preferred_element_type=F32)
    tn = t * lax.rsqrt(ssum * (1.0 / HEAD_DIM) + EPS) * hn
    partner = jnp.where(first_half, pltpu.roll(tn, LANES - HEAD_DIM // 2, 1),
                        pltpu.roll(tn, HEAD_DIM // 2, 1))
    return tn * cos + partner * sin_signed


def _rope_consts(rows):
    lane = lax.broadcasted_iota(jnp.int32, (rows, LANES), 1)
    first_half = (lane % HEAD_DIM) < (HEAD_DIM // 2)
    r = (lax.broadcasted_iota(jnp.int32, (2 * LANES, LANES), 0) % LANES) // HEAD_DIM
    c = lax.broadcasted_iota(jnp.int32, (2 * LANES, LANES), 1) // HEAD_DIM
    ones_blk = jnp.where(r == c, 1.0, 0.0).astype(BF16)
    return first_half, ones_blk


def _kv_kernel(x_ref, g_ref, wk_ref, wv_ref, hn_ref, cos_ref, sin_ref, k_ref, vt_ref):
    x = x_ref[0]
    h = _rms(x, g_ref[...], EPS).astype(BF16)
    first_half, ones_blk = _rope_consts(SEQ_ROWS)
    cos = cos_ref[...]
    sin_signed = sin_ref[...]
    hn = hn_ref[...]
    for cc in range(D_MODEL // PROJ_COLS):
        cols = slice(cc * PROJ_COLS, (cc + 1) * PROJ_COLS)
        k = jnp.dot(h, wk_ref[:, cols], preferred_element_type=F32)
        v = jnp.dot(h, wv_ref[:, cols], preferred_element_type=F32)
        for i in range(PROJ_COLS // LANES):
            c = cc * (PROJ_COLS // LANES) + i
            sl = slice(i * LANES, (i + 1) * LANES)
            kr = _head_norm_rope(k[:, sl], hn, cos, sin_signed, first_half, ones_blk)
            k_ref[0, c] = kr.astype(BF16)
            vt_ref[0, c, 0, :LANES, :] = v[:, sl].T.astype(BF16)
            pad_row = lax.broadcasted_iota(jnp.int32, (VT_ROWS - LANES, SEQ_ROWS), 0)
            vt_ref[0, c, 0, LANES:, :] = jnp.where(pad_row == 0, 1.0, 0.0).astype(BF16)


def _kv(x3d, g, wk, wv, hn, cos, sin_signed):
    b, s, _ = x3d.shape
    nt = s // SEQ_ROWS
    seq_spec = pl.BlockSpec((1, SEQ_ROWS, D_MODEL), lambda bi, si: (bi, si, 0))
    tab = pl.BlockSpec((SEQ_ROWS, LANES), lambda bi, si: (si, 0))
    return pl.pallas_call(
        _kv_kernel,
        out_shape=(jax.ShapeDtypeStruct((b, N_HEADS, s, LANES), BF16),
                   jax.ShapeDtypeStruct((b, N_HEADS, nt, VT_ROWS, SEQ_ROWS), BF16)),
        grid=(b, nt),
        in_specs=[seq_spec, _const_spec((1, D_MODEL)), _const_spec((D_MODEL, D_MODEL)),
                  _const_spec((D_MODEL, D_MODEL)), _const_spec((1, LANES)), tab, tab],
        out_specs=(pl.BlockSpec((1, N_HEADS, SEQ_ROWS, LANES), lambda bi, si: (bi, 0, si, 0)),
                   pl.BlockSpec((1, N_HEADS, 1, VT_ROWS, SEQ_ROWS),
                                lambda bi, si: (bi, 0, si, 0, 0))),
        compiler_params=_params(("arbitrary", "arbitrary")),
        name="kv_proj",
    )(x3d, g, wk, wv, hn, cos, sin_signed)


def _q_kernel(x_ref, g_ref, wq_ref, hn_ref, cos_ref, sin_ref, qt_ref):
    x = x_ref[0]
    h = _rms(x, g_ref[...], EPS).astype(BF16)
    first_half, ones_blk = _rope_consts(SEQ_ROWS)
    cos = cos_ref[...]
    sin_signed = sin_ref[...]
    hn = hn_ref[...]

    def project(cc):
        return jnp.dot(h, wq_ref[:, cc * PROJ_COLS:(cc + 1) * PROJ_COLS],
                       preferred_element_type=F32)

    n_chunks = D_MODEL // PROJ_COLS
    q_next = project(0)
    for cc in range(n_chunks):
        q = q_next
        if cc + 1 < n_chunks:
            q_next = project(cc + 1)
        for i in range(PROJ_COLS // LANES):
            c = cc * (PROJ_COLS // LANES) + i
            qr = _head_norm_rope(q[:, i * LANES:(i + 1) * LANES], hn, cos, sin_signed, first_half,
                                 ones_blk)
            qt_ref[0, c, 0] = (qr * (HEAD_DIM ** -0.5 * LOG2E)).T.astype(BF16)


def _qproj(x3d, layer, j, g, wq, hn, cos, sin_signed):
    b, s, _ = x3d.shape
    nt = s // SEQ_ROWS
    seq_spec = pl.BlockSpec((1, SEQ_ROWS, D_MODEL), lambda bi, si: (bi, si, 0))
    tab = pl.BlockSpec((SEQ_ROWS, LANES), lambda bi, si: (si, 0))
    return pl.pallas_call(
        _q_kernel,
        out_shape=jax.ShapeDtypeStruct((b, N_HEADS, nt, LANES, SEQ_ROWS), BF16),
        grid=(b, nt),
        in_specs=[seq_spec, _layer_spec((1, D_MODEL), layer), _layer_spec((D_MODEL, D_MODEL), j),
                  _layer_spec((1, LANES), j), tab, tab],
        out_specs=pl.BlockSpec((1, N_HEADS, 1, LANES, SEQ_ROWS), lambda bi, si: (bi, 0, si, 0, 0)),
        compiler_params=_params(("arbitrary", "arbitrary")),
        name="q_proj",
    )(x3d, g, wq, hn, cos, sin_signed)


def _attn_kernel(bound_ref, qt_ref, k_ref, vt_ref, lq1_ref, lk1_ref, lq2_ref, lk2_ref, sn_ref,
                 o_ref, s_ref, p_ref, m_ref, acc_ref, *, lambda_init):
    qi = pl.program_id(2)
    bound = bound_ref[0]
    chains = []
    for hh in range(ATT_HEADS):
        qt = qt_ref[0, hh, 0]
        zero = jnp.zeros((HEAD_DIM, ATT_Q), BF16)
        chains += [(hh, jnp.concatenate([qt[:HEAD_DIM], zero], axis=0)),
                   (hh, jnp.concatenate([zero, qt[HEAD_DIM:]], axis=0))]
    n_chain = len(chains)
    n_lane_groups = ATT_Q // LANES

    half_k, half_q = ATT_K // 2, ATT_Q // 2

    def scores(c, j, skip_masked=False):
        hh, qz = chains[c]
        kj = k_ref[0, hh, pl.ds(pl.multiple_of(j * ATT_K, ATT_K), ATT_K), :]
        if skip_masked:
            s_ref[c, :half_k, :] = jnp.dot(kj[:half_k], qz, preferred_element_type=F32)
            s_ref[c, half_k:, half_q:] = jnp.dot(kj[half_k:], qz[:, half_q:],
                                                  preferred_element_type=F32)
        else:
            s_ref[c] = jnp.dot(kj, qz, preferred_element_type=F32)

    def exp_blocks(c, shift, diagonal):
        for rb in range(ATT_K // ATT_ROWS):
            rows = slice(rb * ATT_ROWS, (rb + 1) * ATT_ROWS)
            first_col = ((rb * ATT_ROWS) // CHUNK) * CHUNK if diagonal else 0
            for g in range(n_lane_groups):
                cols = slice(g * LANES, (g + 1) * LANES)
                if (g + 1) * LANES <= first_col:
                    p_ref[c, rows, cols] = jnp.zeros((ATT_ROWS, LANES), BF16)
                    continue
                if jnp.ndim(shift) == 0:
                    sb = shift
                else:
                    sb = jnp.tile(shift[:, cols], (ATT_ROWS // SUBLANES, 1))
                p = jnp.exp2(s_ref[c, rows, cols] - sb)
                if g * LANES < first_col:
                    lane = lax.broadcasted_iota(jnp.int32, (ATT_ROWS, LANES), 1)
                    p = jnp.where(lane >= first_col - g * LANES, p, 0.0)
                p_ref[c, rows, cols] = p.astype(BF16)

    def pv(c, j):
        return jnp.dot(vt_ref[0, chains[c][0], j], p_ref[c], preferred_element_type=F32)

    def bounded_tile(j, diagonal):
        scores(0, j, diagonal)
        scores(1, j, diagonal)
        for c in range(n_chain):
            exp_blocks(c, bound, diagonal)
            if c + 2 < n_chain:
                scores(c + 2, j, diagonal)
            if diagonal:
                vt = vt_ref[0, chains[c][0], j]
                acc_ref[c] = jnp.dot(vt[:, :half_k], p_ref[c, :half_k, :],
                                     preferred_element_type=F32)
                acc_ref[c, :, half_q:] += jnp.dot(vt[:, half_k:], p_ref[c, half_k:, half_q:],
                                                  preferred_element_type=F32)
            else:
                acc_ref[c] += pv(c, j)

    def online_tile(j, diagonal):
        scores(0, j)
        scores(1, j)
        for c in range(n_chain):
            if diagonal:
                k_chunk = lax.broadcasted_iota(jnp.int32, (ATT_K, ATT_Q), 0) // CHUNK
                q_chunk = lax.broadcasted_iota(jnp.int32, (ATT_K, ATT_Q), 1) // CHUNK
                s_all = jnp.where(k_chunk <= q_chunk, s_ref[c], -jnp.inf)
            else:
                s_all = s_ref[c]
            m_old = m_ref[c]
            m_new = jnp.maximum(m_old, jnp.max(s_all, axis=0, keepdims=True))
            alpha = jnp.exp2(m_old - m_new)
            m_ref[c] = m_new
            exp_blocks(c, m_new, diagonal)
            if c + 2 < n_chain:
                scores(c + 2, j)
            acc_ref[c] = jnp.tile(alpha, (VT_ROWS // SUBLANES, 1)) * acc_ref[c] + pv(c, j)

    def attend(tile, diagonal_first):
        def full_tile(j, carry):
            tile(j, False)
            return carry

        if diagonal_first:
            tile(qi, True)
            lax.fori_loop(0, qi, full_tile, 0)
        else:
            acc_ref[...] = jnp.zeros(acc_ref.shape, F32)
            lax.fori_loop(0, qi, full_tile, 0)
            tile(qi, True)

        lam = (jnp.exp(jnp.sum(lq1_ref[...] * lk1_ref[...], axis=-1, keepdims=True))
               - jnp.exp(jnp.sum(lq2_ref[...] * lk2_ref[...], axis=-1, keepdims=True))
               + lambda_init)
        gain = jnp.tile(sn_ref[...], (1, n_lane_groups))
        for hh in range(ATT_HEADS):
            c1, c2 = 2 * hh, 2 * hh + 1
            inv_l1 = 1.0 / acc_ref[c1, LANES:LANES + 1, :]
            inv_l2 = 1.0 / acc_ref[c2, LANES:LANES + 1, :]
            ot = acc_ref[c1, :LANES, :] * inv_l1 - lam * (acc_ref[c2, :LANES, :] * inv_l2)
            ms = jnp.mean(ot * ot, axis=0, keepdims=True)
            yt = ot * lax.rsqrt(ms + SUBLN_EPS) * gain * (1.0 - lambda_init)
            o_ref[0, :, hh * LANES:(hh + 1) * LANES] = yt.T.astype(BF16)

    @pl.when(bound <= STABILIZER_LIMIT)
    def _():
        attend(bounded_tile, diagonal_first=True)

    @pl.when(jnp.logical_not(bound <= STABILIZER_LIMIT))
    def _():
        m_ref[...] = jnp.full(m_ref.shape, -jnp.inf, F32)
        attend(online_tile, diagonal_first=False)


def _attn(qt, k, vt, j, bound, lq1, lk1, lq2, lk2, sn, lambda_init):
    b, nh, s, _ = k.shape
    nt = s // ATT_K
    vec = _const_spec((1, HEAD_DIM))
    lvec = _layer_spec((1, HEAD_DIM), j)
    n_chain = 2 * ATT_HEADS
    return pl.pallas_call(
        functools.partial(_attn_kernel, lambda_init=lambda_init),
        out_shape=jax.ShapeDtypeStruct((b, s, D_MODEL), BF16),
        grid=(b, nh // ATT_HEADS, s // ATT_Q),
        in_specs=[pl.BlockSpec(memory_space=pltpu.SMEM),
                  pl.BlockSpec((1, ATT_HEADS, 1, LANES, ATT_Q),
                               lambda bi, hi, qi: (bi, hi, qi, 0, 0)),
                  pl.BlockSpec((1, ATT_HEADS, s, LANES), lambda bi, hi, qi: (bi, hi, 0, 0)),
                  pl.BlockSpec((1, ATT_HEADS, nt, VT_ROWS, ATT_K),
                               lambda bi, hi, qi: (bi, hi, 0, 0, 0)),
                  lvec, vec, lvec, vec, _layer_spec((LANES, LANES), j)],
        out_specs=pl.BlockSpec((1, ATT_Q, ATT_HEADS * LANES), lambda bi, hi, qi: (bi, qi, hi)),
        scratch_shapes=[pltpu.VMEM((n_chain, ATT_K, ATT_Q), F32),
                        pltpu.VMEM((n_chain, ATT_K, ATT_Q), BF16),
                        pltpu.VMEM((n_chain, SUBLANES, ATT_Q), F32),
                        pltpu.VMEM((n_chain, VT_ROWS, ATT_Q), F32)],
        compiler_params=_params(("arbitrary", "arbitrary", "arbitrary")),
        name="diff_attn",
    )(bound, qt, k, vt, lq1, lk1, lq2, lk2, sn)


def _rope_tables(s):
    pos = jnp.arange(s, dtype=F32)
    inv_freq = ROPE_THETA ** (-jnp.arange(0, HEAD_DIM, 2, dtype=F32) / HEAD_DIM)
    ang = pos[:, None] * inv_freq[None, :]
    ang = jnp.concatenate([ang, ang], axis=-1)
    cos, sin = jnp.cos(ang), jnp.sin(ang)
    half = HEAD_DIM // 2
    sin_signed = jnp.concatenate([-sin[:, :half], sin[:, half:]], axis=-1)
    reps = LANES // HEAD_DIM
    return jnp.tile(cos, (1, reps)), jnp.tile(sin_signed, (1, reps))


def kernel(x, ffn1_norm, ffn1_w_gate, ffn1_w_up, ffn1_w_down, ffn2_norm, ffn2_w_gate, ffn2_w_up, ffn2_w_down, mix_norm, rec_w_in, rec_conv_w, rec_conv_b, rec_w_a, rec_b_a, rec_w_x, rec_b_x, rec_lambda, rec_w_out, kv_norm, w_k, w_v, k_norm, lambda_k1, lambda_k2, attn_w_q, q_norm, lambda_q1, lambda_q2, sub_norm, attn_w_o):
    b, s, d = x.shape
    m = b * s
    reps = LANES // HEAD_DIM
    cos, sin_signed = _rope_tables(s)

    f1 = (ffn1_w_gate, ffn1_w_up, ffn1_w_down)
    f2 = (ffn2_w_gate, ffn2_w_up, ffn2_w_down)
    w_in = rec_w_in.astype(BF16)
    w_ax = jnp.concatenate([rec_w_a, rec_w_x], axis=-1).astype(BF16)
    w_out = rec_w_out.astype(BF16)
    wk, wv = w_k.astype(BF16), w_v.astype(BF16)
    wq, wo = attn_w_q.astype(BF16), attn_w_o.astype(BF16)

    def row(vec):
        return vec.reshape(1, -1)

    def rows(stacked):
        return stacked.reshape(stacked.shape[0], 1, stacked.shape[1])

    g1, g2, gmix = rows(ffn1_norm), rows(ffn2_norm), rows(mix_norm)
    conv_b, b_a, b_x, lam = rows(rec_conv_b), rows(rec_b_a), rows(rec_b_x), rows(rec_lambda)
    q_hn = rows(jnp.tile(q_norm, (1, reps)))
    lq1, lq2 = rows(lambda_q1), rows(lambda_q2)
    sub_g = jnp.broadcast_to(sub_norm[:, :, None], sub_norm.shape + (LANES,))

    k_shared = vt_shared = None
    x2 = x.reshape(m, d)
    for layer in range(DEPTH):
        if layer == N_A_LAYERS:
            k_shared, vt_shared = _kv(x2.reshape(b, s, d), row(kv_norm), wk, wv,
                                      row(jnp.tile(k_norm, reps)), cos, sin_signed)
        x2 = _ffn(x2, layer, g1, *f1)
        if layer < N_A_LAYERS:
            a = layer
            x2 = _rec(x2.reshape(b, s, d), layer, a, gmix, w_in, rec_conv_w, conv_b, w_ax, b_a,
                      b_x, lam, w_out).reshape(m, d)
            x2 = _ffn(x2, layer, g2, *f2)
        else:
            j = layer - N_A_LAYERS
            lambda_init = 0.8 - 0.6 * math.exp(-0.3 * layer)
            qt = _qproj(x2.reshape(b, s, d), layer, j, gmix, wq, q_hn, cos, sin_signed)
            bound = (SCORE_BOUND_MARGIN * HEAD_DIM ** 0.5 * LOG2E
                     * jnp.max(jnp.abs(q_norm[j])) * jnp.max(jnp.abs(k_norm))).reshape(1)
            o = _attn(qt, k_shared, vt_shared, j, bound, lq1, row(lambda_k1), lq2, row(lambda_k2),
                      sub_g, lambda_init)
            x2 = _ffn(x2, layer, g2, *f2, attn=(o.reshape(m, d), wo, j))
    return x2.reshape(b, s, d)
```

```python
import functools
import math

import jax
import jax.numpy as jnp
from jax import lax
from jax.experimental import pallas as pl
from jax.experimental.pallas import tpu as pltpu

F32 = jnp.float32
BF16 = jnp.bfloat16

D_MODEL = 1024
D_FF = 2816
DEPTH = 4
N_A_LAYERS = DEPTH // 2
CHUNK = 64
D_RNN = D_MODEL
N_LRU_BLOCKS = 8
LRU_BLOCK = D_RNN // N_LRU_BLOCKS
CONV_WIDTH = 4
LRU_C = 8.0
N_HEADS = 8
HEAD_DIM = 64
ROPE_THETA = 10000.0
EPS = 1e-6
SUBLN_EPS = 1e-5

LANES = 128
SUBLANES = 8
VMEM_LIMIT = 56 * 1024 * 1024

FFN_ROWS = 1024
FFN_ATTN_ROWS = 512
FFN_COLS = 256
FFN_HEAD_GROUPS = 2
SEQ_ROWS = 512
REC_SEGS = SUBLANES
REC_SEG_LEN = SEQ_ROWS // REC_SEGS
REC_HALO = CONV_WIDTH - 1
REC_SLABS = D_RNN // LANES
PROJ_COLS = 256
ATT_Q = 512
ATT_K = 512
ATT_HEADS = 4
ATT_ROWS = 64
VT_ROWS = 144
LOG2E = 1.4426950408889634
STABILIZER_LIMIT = 50.0
SCORE_BOUND_MARGIN = 1.02


def _rms(xf, g, eps):
    ms = jnp.mean(xf * xf, axis=-1, keepdims=True)
    return xf * lax.rsqrt(ms + eps) * g


def _params(sem):
    return pltpu.CompilerParams(dimension_semantics=sem, vmem_limit_bytes=VMEM_LIMIT)


def _const_spec(shape):
    nd = len(shape)
    return pl.BlockSpec(shape, lambda *_: (0,) * nd, pipeline_mode=pl.Buffered(1))


def _layer_spec(shape, layer):
    nd = len(shape)
    return pl.BlockSpec((None,) + tuple(shape), lambda *_: (layer,) + (0,) * nd,
                        pipeline_mode=pl.Buffered(1))


def _ffn_kernel(*refs, layer, with_attn):
    if with_attn:
        x_ref, ao_ref, wo_ref, g_ref, wg_hbm, wu_hbm, wd_hbm, o_ref = refs[:8]
    else:
        x_ref, g_ref, wg_hbm, wu_hbm, wd_hbm, o_ref = refs[:6]
    act_ref, wg_ref, wu_ref, wd_ref, stage_g, stage_u, stage_d, sems = refs[-8:]
    step = pl.program_id(0)
    n_chunks = D_FF // FFN_COLS

    def chunk_copies(c, slot):
        cols = pl.ds(c * FFN_COLS, FFN_COLS)
        return (pltpu.make_async_copy(wg_hbm.at[layer, :, cols], stage_g.at[slot], sems.at[slot, 0]),
                pltpu.make_async_copy(wu_hbm.at[layer, :, cols], stage_u.at[slot], sems.at[slot, 1]),
                pltpu.make_async_copy(wd_hbm.at[layer, cols, :], stage_d.at[slot], sems.at[slot, 2]))

    def fetch_and_cast(c):
        slot = c % 2
        sl = slice(c * FFN_COLS, (c + 1) * FFN_COLS)
        if c + 1 < n_chunks:
            for cp in chunk_copies(c + 1, 1 - slot):
                cp.start()
        for cp in chunk_copies(c, slot):
            cp.wait()
        wg_ref[:, sl] = stage_g[slot].astype(BF16)
        wu_ref[:, sl] = stage_u[slot].astype(BF16)
        wd_ref[sl, :] = stage_d[slot].astype(BF16)

    def body(load_weights):
        if load_weights:
            for cp in chunk_copies(0, 0):
                cp.start()
        group_rows = x_ref.shape[0] // FFN_HEAD_GROUPS
        xs, hs = [], []
        for r in range(FFN_HEAD_GROUPS):
            rows = slice(r * group_rows, (r + 1) * group_rows)
            xr = x_ref[rows, :]
            if with_attn:
                xr = xr + jnp.dot(ao_ref[rows, :], wo_ref[...], preferred_element_type=F32)
            xs.append(xr)
            hs.append(_rms(xr, g_ref[...], EPS).astype(BF16))
        if load_weights:
            fetch_and_cast(0)
        first = slice(0, FFN_COLS)
        for r in range(FFN_HEAD_GROUPS):
            rows = slice(r * group_rows, (r + 1) * group_rows)
            gate = jnp.dot(hs[r], wg_ref[:, first], preferred_element_type=F32)
            up = jnp.dot(hs[r], wu_ref[:, first], preferred_element_type=F32)
            act_ref[rows, first] = (gate * jax.nn.sigmoid(gate) * up).astype(BF16)
        x = jnp.concatenate(xs, axis=0)
        h = jnp.concatenate(hs, axis=0)
        for c in range(1, n_chunks):
            if load_weights:
                fetch_and_cast(c)
            sl = slice(c * FFN_COLS, (c + 1) * FFN_COLS)
            gate = jnp.dot(h, wg_ref[:, sl], preferred_element_type=F32)
            up = jnp.dot(h, wu_ref[:, sl], preferred_element_type=F32)
            act_ref[:, sl] = (gate * jax.nn.sigmoid(gate) * up).astype(BF16)
        y = jnp.dot(act_ref[...], wd_ref[...], preferred_element_type=F32)
        o_ref[...] = x + 0.5 * y

    @pl.when(step == 0)
    def _():
        body(True)

    @pl.when(step != 0)
    def _():
        body(False)


def _ffn(x2d, layer, g, wg, wu, wd, attn=None):
    m = x2d.shape[0]
    rows = min(m, FFN_ROWS if attn is None else FFN_ATTN_ROWS)
    row_spec = pl.BlockSpec((rows, D_MODEL), lambda i: (i, 0))
    hbm = pl.BlockSpec(memory_space=pl.ANY)
    operands, in_specs = [x2d], [row_spec]
    if attn is not None:
        o2d, wo, j = attn
        operands += [o2d, wo]
        in_specs += [row_spec, _layer_spec((D_MODEL, D_MODEL), j)]
    operands += [g, wg, wu, wd]
    in_specs += [_layer_spec((1, D_MODEL), layer), hbm, hbm, hbm]
    return pl.pallas_call(
        functools.partial(_ffn_kernel, layer=layer, with_attn=attn is not None),
        out_shape=jax.ShapeDtypeStruct((m, D_MODEL), F32),
        grid=(m // rows,),
        in_specs=in_specs,
        out_specs=row_spec,
        scratch_shapes=[pltpu.VMEM((rows, D_FF), BF16),
                        pltpu.VMEM((D_MODEL, D_FF), BF16),
                        pltpu.VMEM((D_MODEL, D_FF), BF16),
                        pltpu.VMEM((D_FF, D_MODEL), BF16),
                        pltpu.VMEM((2, D_MODEL, FFN_COLS), F32),
                        pltpu.VMEM((2, D_MODEL, FFN_COLS), F32),
                        pltpu.VMEM((2, FFN_COLS, D_MODEL), F32),
                        pltpu.SemaphoreType.DMA((2, 3))],
        compiler_params=_params(("arbitrary",)),
        name="ffn_attn_out" if attn is not None else "ffn",
    )(*operands)


def _rec_kernel(x_ref, g_ref, win_ref, cw_ref, cb_ref, wax_ref, ba_ref, bx_ref, lam_ref,
                wout_ref, o_ref, ext_ref, a_ref, b_ref, tail_ref, carry_ref):
    ts = SEQ_ROWS
    halo_rows = REC_HALO * SUBLANES

    @pl.when(pl.program_id(1) == 0)
    def _():
        tail_ref[...] = jnp.zeros(tail_ref.shape, F32)
        carry_ref[...] = jnp.zeros(carry_ref.shape, F32)

    x = x_ref[0]
    h = _rms(x, g_ref[...], EPS).astype(BF16)
    proj = jnp.dot(h, win_ref[...], preferred_element_type=F32)
    gate = proj[:, :D_RNN]

    lam = lam_ref[...]
    neg = -lam
    softplus = jnp.maximum(neg, 0.0) + jnp.log1p(jnp.exp(-jnp.abs(neg)))
    half_c_softplus = (-0.5 * LRU_C) * softplus
    sub = lax.broadcasted_iota(jnp.int32, (SUBLANES, LANES), 0)

    for n in range(REC_SLABS):
        sl = slice(n * LANES, (n + 1) * LANES)
        for j in range(REC_SEGS):
            ext_ref[n, pl.ds(halo_rows + j, REC_SEG_LEN, stride=REC_SEGS), :] = (
                proj[j * REC_SEG_LEN:(j + 1) * REC_SEG_LEN, D_RNN + n * LANES:D_RNN + (n + 1) * LANES])
        for v in range(REC_HALO):
            rows = slice(SUBLANES * v, SUBLANES * (v + 1))
            cur = ext_ref[n, ts + SUBLANES * v:ts + SUBLANES * (v + 1), :]
            prev = tail_ref[n, rows, :]
            ext_ref[n, rows, :] = jnp.where(sub == 0, pltpu.roll(prev, 1, 0), pltpu.roll(cur, 1, 0))
            tail_ref[n, rows, :] = cur

        conv = cb_ref[:, sl]
        for k in range(CONV_WIDTH):
            conv = conv + ext_ref[n, SUBLANES * k:SUBLANES * k + ts, :] * cw_ref[k:k + 1, sl]

        gax = jnp.dot(conv.astype(BF16), wax_ref[n], preferred_element_type=F32)
        t_a = jnp.tanh(0.5 * (gax[:, :LRU_BLOCK] + ba_ref[:, sl]))
        i = 0.5 * jnp.tanh(0.5 * (gax[:, LRU_BLOCK:] + bx_ref[:, sl])) + 0.5
        log_a = half_c_softplus[:, sl] * t_a + half_c_softplus[:, sl]
        a = jnp.exp(log_a)
        a_ref[n] = a
        one_minus_a2 = -jnp.tanh(log_a) * (a * a + 1.0)
        b_ref[n] = jnp.sqrt(one_minus_a2) * (i * conv)

    hs = [jnp.zeros((SUBLANES, LANES), F32)] * REC_SLABS
    ps = [jnp.ones((SUBLANES, LANES), F32)] * REC_SLABS
    for i in range(REC_SEG_LEN):
        rows = slice(SUBLANES * i, SUBLANES * (i + 1))
        for n in range(REC_SLABS):
            a_i = a_ref[n, rows, :]
            hs[n] = a_i * hs[n] + b_ref[n, rows, :]
            ps[n] = a_i * ps[n]
            b_ref[n, rows, :] = hs[n]
            a_ref[n, rows, :] = ps[n]

    cols = []
    for n in range(REC_SLABS):
        carry_in = carry_ref[n]
        c = carry_in
        for _ in range(REC_SEGS - 1):
            c = jnp.where(sub == 0, carry_in, pltpu.roll(hs[n] + ps[n] * c, 1, 0))
        seg_end = hs[n] + ps[n] * c
        carry_ref[n] = jnp.broadcast_to(seg_end[SUBLANES - 1:SUBLANES, :], (SUBLANES, LANES))
        b_ref[n] = b_ref[n] + a_ref[n] * jnp.tile(c, (REC_SEG_LEN, 1))
        cols.append(jnp.concatenate(
            [b_ref[n, pl.ds(j, REC_SEG_LEN, stride=REC_SEGS), :] for j in range(REC_SEGS)], axis=0))
    h_scan = jnp.concatenate(cols, axis=1)

    y = (jax.nn.gelu(gate, approximate=True) * h_scan).astype(BF16)
    o_ref[0] = x + jnp.dot(y, wout_ref[...], preferred_element_type=F32)


def _rec(x3d, layer, a, g, win, cw, cb, wax, ba, bx, lam, wout):
    b, s, _ = x3d.shape
    seq_spec = pl.BlockSpec((1, SEQ_ROWS, D_MODEL), lambda bi, si: (bi, si, 0))
    vec = _layer_spec((1, D_RNN), a)
    return pl.pallas_call(
        _rec_kernel,
        out_shape=jax.ShapeDtypeStruct((b, s, D_MODEL), F32),
        grid=(b, s // SEQ_ROWS),
        in_specs=[seq_spec, _layer_spec((1, D_MODEL), layer),
                  _layer_spec((D_MODEL, 2 * D_RNN), a),
                  _layer_spec((CONV_WIDTH, D_RNN), a), vec,
                  _layer_spec((N_LRU_BLOCKS, LRU_BLOCK, 2 * LRU_BLOCK), a), vec, vec, vec,
                  _layer_spec((D_RNN, D_MODEL), a)],
        out_specs=seq_spec,
        scratch_shapes=[pltpu.VMEM((REC_SLABS, SEQ_ROWS + REC_HALO * SUBLANES, LANES), F32),
                        pltpu.VMEM((REC_SLABS, SEQ_ROWS, LANES), F32),
                        pltpu.VMEM((REC_SLABS, SEQ_ROWS, LANES), F32),
                        pltpu.VMEM((REC_SLABS, REC_HALO * SUBLANES, LANES), F32),
                        pltpu.VMEM((REC_SLABS, SUBLANES, LANES), F32)],
        compiler_params=_params(("arbitrary", "arbitrary")),
        name="rec_block",
    )(x3d, g, win, cw, cb, wax, ba, bx, lam, wout)


def _head_norm_rope(t, hn, cos, sin_signed, first_half, ones_blk):
    sq = t * t
    hi = sq.astype(BF16)
    lo = (sq - hi.astype(F32)).astype(BF16)
    ssum = jnp.dot(jnp.concatenate([hi, lo], axis=1), ones_blk, preferred_element_type=F32)
    tn = t * lax.rsqrt(ssum * (1.0 / HEAD_DIM) + EPS) * hn
    partner = jnp.where(first_half, pltpu.roll(tn, LANES - HEAD_DIM // 2, 1),
                        pltpu.roll(tn, HEAD_DIM // 2, 1))
    return tn * cos + partner * sin_signed


def _rope_consts(rows):
    lane = lax.broadcasted_iota(jnp.int32, (rows, LANES), 1)
    first_half = (lane % HEAD_DIM) < (HEAD_DIM // 2)
    r = (lax.broadcasted_iota(jnp.int32, (2 * LANES, LANES), 0) % LANES) // HEAD_DIM
    c = lax.broadcasted_iota(jnp.int32, (2 * LANES, LANES), 1) // HEAD_DIM
    ones_blk = jnp.where(r == c, 1.0, 0.0).astype(BF16)
    return first_half, ones_blk


def _kv_kernel(x_ref, g_ref, wk_ref, wv_ref, hn_ref, cos_ref, sin_ref, k_ref, vt_ref):
    x = x_ref[0]
    h = _rms(x, g_ref[...], EPS).astype(BF16)
    first_half, ones_blk = _rope_consts(SEQ_ROWS)
    cos = cos_ref[...]
    sin_signed = sin_ref[...]
    hn = hn_ref[...]
    for cc in range(D_MODEL // PROJ_COLS):
        cols = slice(cc * PROJ_COLS, (cc + 1) * PROJ_COLS)
        k = jnp.dot(h, wk_ref[:, cols], preferred_element_type=F32)
        v = jnp.dot(h, wv_ref[:, cols], preferred_element_type=F32)
        for i in range(PROJ_COLS // LANES):
            c = cc * (PROJ_COLS // LANES) + i
            sl = slice(i * LANES, (i + 1) * LANES)
            kr = _head_norm_rope(k[:, sl], hn, cos, sin_signed, first_half, ones_blk)
            k_ref[0, c] = kr.astype(BF16)
            vt_ref[0, c, 0, :LANES, :] = v[:, sl].T.astype(BF16)
            pad_row = lax.broadcasted_iota(jnp.int32, (VT_ROWS - LANES, SEQ_ROWS), 0)
            vt_ref[0, c, 0, LANES:, :] = jnp.where(pad_row == 0, 1.0, 0.0).astype(BF16)


def _kv(x3d, g, wk, wv, hn, cos, sin_signed):
    b, s, _ = x3d.shape
    nt = s // SEQ_ROWS
    seq_spec = pl.BlockSpec((1, SEQ_ROWS, D_MODEL), lambda bi, si: (bi, si, 0))
    tab = pl.BlockSpec((SEQ_ROWS, LANES), lambda bi, si: (si, 0))
    return pl.pallas_call(
        _kv_kernel,
        out_shape=(jax.ShapeDtypeStruct((b, N_HEADS, s, LANES), BF16),
                   jax.ShapeDtypeStruct((b, N_HEADS, nt, VT_ROWS, SEQ_ROWS), BF16)),
        grid=(b, nt),
        in_specs=[seq_spec, _const_spec((1, D_MODEL)), _const_spec((D_MODEL, D_MODEL)),
                  _const_spec((D_MODEL, D_MODEL)), _const_spec((1, LANES)), tab, tab],
        out_specs=(pl.BlockSpec((1, N_HEADS, SEQ_ROWS, LANES), lambda bi, si: (bi, 0, si, 0)),
                   pl.BlockSpec((1, N_HEADS, 1, VT_ROWS, SEQ_ROWS),
                                lambda bi, si: (bi, 0, si, 0, 0))),
        compiler_params=_params(("arbitrary", "arbitrary")),
        name="kv_proj",
    )(x3d, g, wk, wv, hn, cos, sin_signed)


def _q_kernel(x_ref, g_ref, wqt_ref, hn_ref, cos_ref, sin_ref, qt_ref):
    x = x_ref[0]
    ht = _rms(x, g_ref[...], EPS).T.astype(BF16)
    cos = cos_ref[...]
    sin_signed = sin_ref[...]
    hn = jnp.tile(hn_ref[...], (1, SEQ_ROWS // LANES))
    half = HEAD_DIM // 2

    def project(cc):
        return jnp.dot(wqt_ref[cc * PROJ_COLS:(cc + 1) * PROJ_COLS, :], ht,
                       preferred_element_type=F32)

    n_chunks = D_MODEL // PROJ_COLS
    q_next = project(0)
    for cc in range(n_chunks):
        q = q_next
        if cc + 1 < n_chunks:
            q_next = project(cc + 1)
        for i in range(PROJ_COLS // LANES):
            c = cc * (PROJ_COLS // LANES) + i
            parts = []
            for comp in range(LANES // HEAD_DIM):
                t = q[i * LANES + comp * HEAD_DIM:i * LANES + (comp + 1) * HEAD_DIM, :]
                ms = jnp.mean(t * t, axis=0, keepdims=True)
                rows = slice(comp * HEAD_DIM, (comp + 1) * HEAD_DIM)
                tn = t * lax.rsqrt(ms + EPS) * hn[rows, :]
                partner = jnp.concatenate([tn[half:], tn[:half]], axis=0)
                parts.append(tn * cos[rows, :] + partner * sin_signed[rows, :])
            qr = jnp.concatenate(parts, axis=0)
            qt_ref[0, c, 0] = (qr * (HEAD_DIM ** -0.5 * LOG2E)).astype(BF16)


def _qproj(x3d, layer, j, g, wqt, hn, cos_t, sin_t):
    b, s, _ = x3d.shape
    nt = s // SEQ_ROWS
    seq_spec = pl.BlockSpec((1, SEQ_ROWS, D_MODEL), lambda bi, si: (bi, si, 0))
    tab = pl.BlockSpec((LANES, SEQ_ROWS), lambda bi, si: (0, si))
    return pl.pallas_call(
        _q_kernel,
        out_shape=jax.ShapeDtypeStruct((b, N_HEADS, nt, LANES, SEQ_ROWS), BF16),
        grid=(b, nt),
        in_specs=[seq_spec, _layer_spec((1, D_MODEL), layer), _layer_spec((D_MODEL, D_MODEL), j),
                  _layer_spec((LANES, LANES), j), tab, tab],
        out_specs=pl.BlockSpec((1, N_HEADS, 1, LANES, SEQ_ROWS), lambda bi, si: (bi, 0, si, 0, 0)),
        compiler_params=_params(("arbitrary", "arbitrary")),
        name="q_proj",
    )(x3d, g, wqt, hn, cos_t, sin_t)


def _attn_kernel(bound_ref, qt_ref, k_ref, vt_ref, lq1_ref, lk1_ref, lq2_ref, lk2_ref, sn_ref,
                 o_ref, s_ref, p_ref, m_ref, acc_ref, *, lambda_init):
    qi = pl.program_id(2)
    bound = bound_ref[0]
    chains = []
    for hh in range(ATT_HEADS):
        qt = qt_ref[0, hh, 0]
        zero = jnp.zeros((HEAD_DIM, ATT_Q), BF16)
        chains += [(hh, jnp.concatenate([qt[:HEAD_DIM], zero], axis=0)),
                   (hh, jnp.concatenate([zero, qt[HEAD_DIM:]], axis=0))]
    n_chain = len(chains)
    n_lane_groups = ATT_Q // LANES

    half_k, half_q = ATT_K // 2, ATT_Q // 2

    def scores(c, j, skip_masked=False):
        hh, qz = chains[c]
        kj = k_ref[0, hh, pl.ds(pl.multiple_of(j * ATT_K, ATT_K), ATT_K), :]
        if skip_masked:
            s_ref[c, :half_k, :] = jnp.dot(kj[:half_k], qz, preferred_element_type=F32)
            s_ref[c, half_k:, half_q:] = jnp.dot(kj[half_k:], qz[:, half_q:],
                                                  preferred_element_type=F32)
        else:
            s_ref[c] = jnp.dot(kj, qz, preferred_element_type=F32)

    def exp_blocks(c, shift, diagonal):
        for rb in range(ATT_K // ATT_ROWS):
            rows = slice(rb * ATT_ROWS, (rb + 1) * ATT_ROWS)
            first_col = ((rb * ATT_ROWS) // CHUNK) * CHUNK if diagonal else 0
            for g in range(n_lane_groups):
                cols = slice(g * LANES, (g + 1) * LANES)
                if (g + 1) * LANES <= first_col:
                    p_ref[c, rows, cols] = jnp.zeros((ATT_ROWS, LANES), BF16)
                    continue
                if jnp.ndim(shift) == 0:
                    sb = shift
                else:
                    sb = jnp.tile(shift[:, cols], (ATT_ROWS // SUBLANES, 1))
                p = jnp.exp2(s_ref[c, rows, cols] - sb)
                if g * LANES < first_col:
                    lane = lax.broadcasted_iota(jnp.int32, (ATT_ROWS, LANES), 1)
                    p = jnp.where(lane >= first_col - g * LANES, p, 0.0)
                p_ref[c, rows, cols] = p.astype(BF16)

    def pv(c, j):
        return jnp.dot(vt_ref[0, chains[c][0], j], p_ref[c], preferred_element_type=F32)

    def bounded_tile(j, diagonal):
        scores(0, j, diagonal)
        scores(1, j, diagonal)
        for c in range(n_chain):
            exp_blocks(c, bound, diagonal)
            if c + 2 < n_chain:
                scores(c + 2, j, diagonal)
            if diagonal:
                vt = vt_ref[0, chains[c][0], j]
                acc_ref[c] = jnp.dot(vt[:, :half_k], p_ref[c, :half_k, :],
                                     preferred_element_type=F32)
                acc_ref[c, :, half_q:] += jnp.dot(vt[:, half_k:], p_ref[c, half_k:, half_q:],
                                                  preferred_element_type=F32)
            else:
                acc_ref[c] += pv(c, j)

    def online_tile(j, diagonal):
        scores(0, j)
        scores(1, j)
        for c in range(n_chain):
            if diagonal:
                k_chunk = lax.broadcasted_iota(jnp.int32, (ATT_K, ATT_Q), 0) // CHUNK
                q_chunk = lax.broadcasted_iota(jnp.int32, (ATT_K, ATT_Q), 1) // CHUNK
                s_all = jnp.where(k_chunk <= q_chunk, s_ref[c], -jnp.inf)
            else:
                s_all = s_ref[c]
            m_old = m_ref[c]
            m_new = jnp.maximum(m_old, jnp.max(s_all, axis=0, keepdims=True))
            alpha = jnp.exp2(m_old - m_new)
            m_ref[c] = m_new
            exp_blocks(c, m_new, diagonal)
            if c + 2 < n_chain:
                scores(c + 2, j)
            acc_ref[c] = jnp.tile(alpha, (VT_ROWS // SUBLANES, 1)) * acc_ref[c] + pv(c, j)

    def attend(tile, diagonal_first):
        def full_tile(j, carry):
            tile(j, False)
            return carry

        if diagonal_first:
            tile(qi, True)
            lax.fori_loop(0, qi, full_tile, 0)
        else:
            acc_ref[...] = jnp.zeros(acc_ref.shape, F32)
            lax.fori_loop(0, qi, full_tile, 0)
            tile(qi, True)

        lam = (jnp.exp(jnp.sum(lq1_ref[...] * lk1_ref[...], axis=-1, keepdims=True))
               - jnp.exp(jnp.sum(lq2_ref[...] * lk2_ref[...], axis=-1, keepdims=True))
               + lambda_init)
        gain = jnp.tile(sn_ref[...], (1, n_lane_groups))
        for hh in range(ATT_HEADS):
            c1, c2 = 2 * hh, 2 * hh + 1
            inv_l1 = 1.0 / acc_ref[c1, LANES:LANES + 1, :]
            inv_l2 = 1.0 / acc_ref[c2, LANES:LANES + 1, :]
            ot = acc_ref[c1, :LANES, :] * inv_l1 - lam * (acc_ref[c2, :LANES, :] * inv_l2)
            ms = jnp.mean(ot * ot, axis=0, keepdims=True)
            yt = ot * lax.rsqrt(ms + SUBLN_EPS) * gain * (1.0 - lambda_init)
            o_ref[0, :, hh * LANES:(hh + 1) * LANES] = yt.T.astype(BF16)

    @pl.when(bound <= STABILIZER_LIMIT)
    def _():
        attend(bounded_tile, diagonal_first=True)

    @pl.when(jnp.logical_not(bound <= STABILIZER_LIMIT))
    def _():
        m_ref[...] = jnp.full(m_ref.shape, -jnp.inf, F32)
        attend(online_tile, diagonal_first=False)


def _attn(qt, k, vt, j, bound, lq1, lk1, lq2, lk2, sn, lambda_init):
    b, nh, s, _ = k.shape
    nt = s // ATT_K
    vec = _const_spec((1, HEAD_DIM))
    lvec = _layer_spec((1, HEAD_DIM), j)
    n_chain = 2 * ATT_HEADS
    return pl.pallas_call(
        functools.partial(_attn_kernel, lambda_init=lambda_init),
        out_shape=jax.ShapeDtypeStruct((b, s, D_MODEL), BF16),
        grid=(b, nh // ATT_HEADS, s // ATT_Q),
        in_specs=[pl.BlockSpec(memory_space=pltpu.SMEM),
                  pl.BlockSpec((1, ATT_HEADS, 1, LANES, ATT_Q),
                               lambda bi, hi, qi: (bi, hi, qi, 0, 0)),
                  pl.BlockSpec((1, ATT_HEADS, s, LANES), lambda bi, hi, qi: (bi, hi, 0, 0)),
                  pl.BlockSpec((1, ATT_HEADS, nt, VT_ROWS, ATT_K),
                               lambda bi, hi, qi: (bi, hi, 0, 0, 0)),
                  lvec, vec, lvec, vec, _layer_spec((LANES, LANES), j)],
        out_specs=pl.BlockSpec((1, ATT_Q, ATT_HEADS * LANES), lambda bi, hi, qi: (bi, qi, hi)),
        scratch_shapes=[pltpu.VMEM((n_chain, ATT_K, ATT_Q), F32),
                        pltpu.VMEM((n_chain, ATT_K, ATT_Q), BF16),
                        pltpu.VMEM((n_chain, SUBLANES, ATT_Q), F32),
                        pltpu.VMEM((n_chain, VT_ROWS, ATT_Q), F32)],
        compiler_params=_params(("arbitrary", "arbitrary", "arbitrary")),
        name="diff_attn",
    )(bound, qt, k, vt, lq1, lk1, lq2, lk2, sn)


def _rope_tables(s):
    pos = jnp.arange(s, dtype=F32)
    inv_freq = ROPE_THETA ** (-jnp.arange(0, HEAD_DIM, 2, dtype=F32) / HEAD_DIM)
    ang = pos[:, None] * inv_freq[None, :]
    ang = jnp.concatenate([ang, ang], axis=-1)
    cos, sin = jnp.cos(ang), jnp.sin(ang)
    half = HEAD_DIM // 2
    sin_signed = jnp.concatenate([-sin[:, :half], sin[:, half:]], axis=-1)
    reps = LANES // HEAD_DIM
    return jnp.tile(cos, (1, reps)), jnp.tile(sin_signed, (1, reps))


def kernel(x, ffn1_norm, ffn1_w_gate, ffn1_w_up, ffn1_w_down, ffn2_norm, ffn2_w_gate, ffn2_w_up, ffn2_w_down, mix_norm, rec_w_in, rec_conv_w, rec_conv_b, rec_w_a, rec_b_a, rec_w_x, rec_b_x, rec_lambda, rec_w_out, kv_norm, w_k, w_v, k_norm, lambda_k1, lambda_k2, attn_w_q, q_norm, lambda_q1, lambda_q2, sub_norm, attn_w_o):
    b, s, d = x.shape
    m = b * s
    reps = LANES // HEAD_DIM
    cos, sin_signed = _rope_tables(s)

    f1 = (ffn1_w_gate, ffn1_w_up, ffn1_w_down)
    f2 = (ffn2_w_gate, ffn2_w_up, ffn2_w_down)
    w_in = rec_w_in.astype(BF16)
    w_ax = jnp.concatenate([rec_w_a, rec_w_x], axis=-1).astype(BF16)
    w_out = rec_w_out.astype(BF16)
    wk, wv = w_k.astype(BF16), w_v.astype(BF16)
    wqt = attn_w_q.astype(BF16).transpose(0, 2, 1)
    wo = attn_w_o.astype(BF16)

    def row(vec):
        return vec.reshape(1, -1)

    def rows(stacked):
        return stacked.reshape(stacked.shape[0], 1, stacked.shape[1])

    g1, g2, gmix = rows(ffn1_norm), rows(ffn2_norm), rows(mix_norm)
    conv_b, b_a, b_x, lam = rows(rec_conv_b), rows(rec_b_a), rows(rec_b_x), rows(rec_lambda)
    q_hn = jnp.broadcast_to(jnp.tile(q_norm, (1, reps))[:, :, None], (q_norm.shape[0], LANES, LANES))
    cos_t, sin_t = cos.T, sin_signed.T
    lq1, lq2 = rows(lambda_q1), rows(lambda_q2)
    sub_g = jnp.broadcast_to(sub_norm[:, :, None], sub_norm.shape + (LANES,))

    k_shared = vt_shared = None
    x2 = x.reshape(m, d)
    for layer in range(DEPTH):
        if layer == N_A_LAYERS:
            k_shared, vt_shared = _kv(x2.reshape(b, s, d), row(kv_norm), wk, wv,
                                      row(jnp.tile(k_norm, reps)), cos, sin_signed)
        x2 = _ffn(x2, layer, g1, *f1)
        if layer < N_A_LAYERS:
            a = layer
            x2 = _rec(x2.reshape(b, s, d), layer, a, gmix, w_in, rec_conv_w, conv_b, w_ax, b_a,
                      b_x, lam, w_out).reshape(m, d)
            x2 = _ffn(x2, layer, g2, *f2)
        else:
            j = layer - N_A_LAYERS
            lambda_init = 0.8 - 0.6 * math.exp(-0.3 * layer)
            qt = _qproj(x2.reshape(b, s, d), layer, j, gmix, wqt, q_hn, cos_t, sin_t)
            bound = (SCORE_BOUND_MARGIN * HEAD_DIM ** 0.5 * LOG2E
                     * jnp.max(jnp.abs(q_norm[j])) * jnp.max(jnp.abs(k_norm))).reshape(1)
            o = _attn(qt, k_shared, vt_shared, j, bound, lq1, row(lambda_k1), lq2, row(lambda_k2),
                      sub_g, lambda_init)
            x2 = _ffn(x2, layer, g2, *f2, attn=(o.reshape(m, d), wo, j))
    return x2.reshape(b, s, d)
```

```python
import functools
import math

import jax
import jax.numpy as jnp
from jax import lax
from jax.experimental import pallas as pl
from jax.experimental.pallas import tpu as pltpu

F32 = jnp.float32
BF16 = jnp.bfloat16

D_MODEL = 1024
D_FF = 2816
DEPTH = 4
N_A_LAYERS = DEPTH // 2
CHUNK = 64
D_RNN = D_MODEL
N_LRU_BLOCKS = 8
LRU_BLOCK = D_RNN // N_LRU_BLOCKS
CONV_WIDTH = 4
LRU_C = 8.0
N_HEADS = 8
HEAD_DIM = 64
ROPE_THETA = 10000.0
EPS = 1e-6
SUBLN_EPS = 1e-5

LANES = 128
SUBLANES = 8
VMEM_LIMIT = 56 * 1024 * 1024

FFN_ROWS = 512
FFN_ATTN_ROWS = 512
FFN_COLS = 256
FFN_HEAD_GROUPS = 2
SEQ_ROWS = 512
REC_SEGS = SUBLANES
REC_SEG_LEN = SEQ_ROWS // REC_SEGS
REC_HALO = CONV_WIDTH - 1
REC_SLABS = D_RNN // LANES
PROJ_COLS = 256
ATT_Q = 512
ATT_K = 512
ATT_HEADS = 4
ATT_ROWS = 64
VT_ROWS = 144
LOG2E = 1.4426950408889634
STABILIZER_LIMIT = 50.0
SCORE_BOUND_MARGIN = 1.02


def _rms(xf, g, eps):
    ms = jnp.mean(xf * xf, axis=-1, keepdims=True)
    return xf * lax.rsqrt(ms + eps) * g


def _params(sem):
    return pltpu.CompilerParams(dimension_semantics=sem, vmem_limit_bytes=VMEM_LIMIT)


def _const_spec(shape):
    nd = len(shape)
    return pl.BlockSpec(shape, lambda *_: (0,) * nd, pipeline_mode=pl.Buffered(1))


def _layer_spec(shape, layer):
    nd = len(shape)
    return pl.BlockSpec((None,) + tuple(shape), lambda *_: (layer,) + (0,) * nd,
                        pipeline_mode=pl.Buffered(1))


def _ffn_kernel(*refs, layer, with_attn):
    if with_attn:
        x_ref, ao_ref, wo_ref, g_ref, wg_hbm, wu_hbm, wd_hbm, o_ref = refs[:8]
    else:
        x_ref, g_ref, wg_hbm, wu_hbm, wd_hbm, o_ref = refs[:6]
    act_ref, wg_ref, wu_ref, wd_ref, stage_g, stage_u, stage_d, sems = refs[-8:]
    step = pl.program_id(0)
    n_chunks = D_FF // FFN_COLS

    def chunk_copies(c, slot):
        cols = pl.ds(c * FFN_COLS, FFN_COLS)
        return (pltpu.make_async_copy(wg_hbm.at[layer, :, cols], stage_g.at[slot], sems.at[slot, 0]),
                pltpu.make_async_copy(wu_hbm.at[layer, :, cols], stage_u.at[slot], sems.at[slot, 1]),
                pltpu.make_async_copy(wd_hbm.at[layer, cols, :], stage_d.at[slot], sems.at[slot, 2]))

    def fetch_and_cast(c):
        slot = c % 2
        sl = slice(c * FFN_COLS, (c + 1) * FFN_COLS)
        if c + 1 < n_chunks:
            for cp in chunk_copies(c + 1, 1 - slot):
                cp.start()
        for cp in chunk_copies(c, slot):
            cp.wait()
        wg_ref[:, sl] = stage_g[slot].astype(BF16)
        wu_ref[:, sl] = stage_u[slot].astype(BF16)
        wd_ref[sl, :] = stage_d[slot].astype(BF16)

    def body(load_weights):
        if load_weights:
            for cp in chunk_copies(0, 0):
                cp.start()
        group_rows = x_ref.shape[0] // FFN_HEAD_GROUPS
        xs, hs = [], []
        for r in range(FFN_HEAD_GROUPS):
            rows = slice(r * group_rows, (r + 1) * group_rows)
            xr = x_ref[rows, :]
            if with_attn:
                xr = xr + jnp.dot(ao_ref[rows, :], wo_ref[...], preferred_element_type=F32)
            xs.append(xr)
            hs.append(_rms(xr, g_ref[...], EPS).astype(BF16))
        if load_weights:
            fetch_and_cast(0)
        first = slice(0, FFN_COLS)
        for r in range(FFN_HEAD_GROUPS):
            rows = slice(r * group_rows, (r + 1) * group_rows)
            gate = jnp.dot(hs[r], wg_ref[:, first], preferred_element_type=F32)
            up = jnp.dot(hs[r], wu_ref[:, first], preferred_element_type=F32)
            act_ref[rows, first] = (gate * jax.nn.sigmoid(gate) * up).astype(BF16)
        x = jnp.concatenate(xs, axis=0)
        h = jnp.concatenate(hs, axis=0)
        for c in range(1, n_chunks):
            if load_weights:
                fetch_and_cast(c)
            sl = slice(c * FFN_COLS, (c + 1) * FFN_COLS)
            gate = jnp.dot(h, wg_ref[:, sl], preferred_element_type=F32)
            up = jnp.dot(h, wu_ref[:, sl], preferred_element_type=F32)
            act_ref[:, sl] = (gate * jax.nn.sigmoid(gate) * up).astype(BF16)
        y = jnp.dot(act_ref[...], wd_ref[...], preferred_element_type=F32)
        o_ref[...] = x + 0.5 * y

    @pl.when(step == 0)
    def _():
        body(True)

    @pl.when(step != 0)
    def _():
        body(False)


def _ffn(x2d, layer, g, wg, wu, wd, attn=None):
    m = x2d.shape[0]
    rows = min(m, FFN_ROWS if attn is None else FFN_ATTN_ROWS)
    row_spec = pl.BlockSpec((rows, D_MODEL), lambda i: (i, 0))
    hbm = pl.BlockSpec(memory_space=pl.ANY)
    operands, in_specs = [x2d], [row_spec]
    if attn is not None:
        o2d, wo, j = attn
        operands += [o2d, wo]
        in_specs += [row_spec, _layer_spec((D_MODEL, D_MODEL), j)]
    operands += [g, wg, wu, wd]
    in_specs += [_layer_spec((1, D_MODEL), layer), hbm, hbm, hbm]
    return pl.pallas_call(
        functools.partial(_ffn_kernel, layer=layer, with_attn=attn is not None),
        out_shape=jax.ShapeDtypeStruct((m, D_MODEL), F32),
        grid=(m // rows,),
        in_specs=in_specs,
        out_specs=row_spec,
        scratch_shapes=[pltpu.VMEM((rows, D_FF), BF16),
                        pltpu.VMEM((D_MODEL, D_FF), BF16),
                        pltpu.VMEM((D_MODEL, D_FF), BF16),
                        pltpu.VMEM((D_FF, D_MODEL), BF16),
                        pltpu.VMEM((2, D_MODEL, FFN_COLS), F32),
                        pltpu.VMEM((2, D_MODEL, FFN_COLS), F32),
                        pltpu.VMEM((2, FFN_COLS, D_MODEL), F32),
                        pltpu.SemaphoreType.DMA((2, 3))],
        compiler_params=_params(("arbitrary",)),
        name="ffn_attn_out" if attn is not None else "ffn",
    )(*operands)


def _rec_kernel(x_ref, g_ref, win_ref, cw_ref, cb_ref, wax_ref, ba_ref, bx_ref, lam_ref,
                wout_ref, o_ref, ext_ref, a_ref, b_ref, tail_ref, carry_ref):
    ts = SEQ_ROWS
    halo_rows = REC_HALO * SUBLANES

    @pl.when(pl.program_id(1) == 0)
    def _():
        tail_ref[...] = jnp.zeros(tail_ref.shape, F32)
        carry_ref[...] = jnp.zeros(carry_ref.shape, F32)

    x = x_ref[0]
    h = _rms(x, g_ref[...], EPS).astype(BF16)
    proj = jnp.dot(h, win_ref[...], preferred_element_type=F32)
    gate = proj[:, :D_RNN]

    lam = lam_ref[...]
    neg = -lam
    softplus = jnp.maximum(neg, 0.0) + jnp.log1p(jnp.exp(-jnp.abs(neg)))
    half_c_softplus = (-0.5 * LRU_C) * softplus
    sub = lax.broadcasted_iota(jnp.int32, (SUBLANES, LANES), 0)

    for n in range(REC_SLABS):
        sl = slice(n * LANES, (n + 1) * LANES)
        for j in range(REC_SEGS):
            ext_ref[n, pl.ds(halo_rows + j, REC_SEG_LEN, stride=REC_SEGS), :] = (
                proj[j * REC_SEG_LEN:(j + 1) * REC_SEG_LEN, D_RNN + n * LANES:D_RNN + (n + 1) * LANES])
        for v in range(REC_HALO):
            rows = slice(SUBLANES * v, SUBLANES * (v + 1))
            cur = ext_ref[n, ts + SUBLANES * v:ts + SUBLANES * (v + 1), :]
            prev = tail_ref[n, rows, :]
            ext_ref[n, rows, :] = jnp.where(sub == 0, pltpu.roll(prev, 1, 0), pltpu.roll(cur, 1, 0))
            tail_ref[n, rows, :] = cur

        conv = cb_ref[:, sl]
        for k in range(CONV_WIDTH):
            conv = conv + ext_ref[n, SUBLANES * k:SUBLANES * k + ts, :] * cw_ref[k:k + 1, sl]

        gax = jnp.dot(conv.astype(BF16), wax_ref[n], preferred_element_type=F32)
        t_a = jnp.tanh(0.5 * (gax[:, :LRU_BLOCK] + ba_ref[:, sl]))
        i = 0.5 * jnp.tanh(0.5 * (gax[:, LRU_BLOCK:] + bx_ref[:, sl])) + 0.5
        log_a = half_c_softplus[:, sl] * t_a + half_c_softplus[:, sl]
        a = jnp.exp(log_a)
        a_ref[n] = a
        one_minus_a2 = -jnp.tanh(log_a) * (a * a + 1.0)
        b_ref[n] = jnp.sqrt(one_minus_a2) * (i * conv)

    hs = [jnp.zeros((SUBLANES, LANES), F32)] * REC_SLABS
    ps = [jnp.ones((SUBLANES, LANES), F32)] * REC_SLABS
    for i in range(REC_SEG_LEN):
        rows = slice(SUBLANES * i, SUBLANES * (i + 1))
        for n in range(REC_SLABS):
            a_i = a_ref[n, rows, :]
            hs[n] = a_i * hs[n] + b_ref[n, rows, :]
            ps[n] = a_i * ps[n]
            b_ref[n, rows, :] = hs[n]
            a_ref[n, rows, :] = ps[n]

    cols = []
    for n in range(REC_SLABS):
        carry_in = carry_ref[n]
        c = carry_in
        for _ in range(REC_SEGS - 1):
            c = jnp.where(sub == 0, carry_in, pltpu.roll(hs[n] + ps[n] * c, 1, 0))
        seg_end = hs[n] + ps[n] * c
        carry_ref[n] = jnp.broadcast_to(seg_end[SUBLANES - 1:SUBLANES, :], (SUBLANES, LANES))
        b_ref[n] = b_ref[n] + a_ref[n] * jnp.tile(c, (REC_SEG_LEN, 1))
        cols.append(jnp.concatenate(
            [b_ref[n, pl.ds(j, REC_SEG_LEN, stride=REC_SEGS), :] for j in range(REC_SEGS)], axis=0))
    h_scan = jnp.concatenate(cols, axis=1)

    y = (jax.nn.gelu(gate, approximate=True) * h_scan).astype(BF16)
    o_ref[0] = x + jnp.dot(y, wout_ref[...], preferred_element_type=F32)


def _rec(x3d, layer, a, g, win, cw, cb, wax, ba, bx, lam, wout):
    b, s, _ = x3d.shape
    seq_spec = pl.BlockSpec((1, SEQ_ROWS, D_MODEL), lambda bi, si: (bi, si, 0))
    vec = _layer_spec((1, D_RNN), a)
    return pl.pallas_call(
        _rec_kernel,
        out_shape=jax.ShapeDtypeStruct((b, s, D_MODEL), F32),
        grid=(b, s // SEQ_ROWS),
        in_specs=[seq_spec, _layer_spec((1, D_MODEL), layer),
                  _layer_spec((D_MODEL, 2 * D_RNN), a),
                  _layer_spec((CONV_WIDTH, D_RNN), a), vec,
                  _layer_spec((N_LRU_BLOCKS, LRU_BLOCK, 2 * LRU_BLOCK), a), vec, vec, vec,
                  _layer_spec((D_RNN, D_MODEL), a)],
        out_specs=seq_spec,
        scratch_shapes=[pltpu.VMEM((REC_SLABS, SEQ_ROWS + REC_HALO * SUBLANES, LANES), F32),
                        pltpu.VMEM((REC_SLABS, SEQ_ROWS, LANES), F32),
                        pltpu.VMEM((REC_SLABS, SEQ_ROWS, LANES), F32),
                        pltpu.VMEM((REC_SLABS, REC_HALO * SUBLANES, LANES), F32),
                        pltpu.VMEM((REC_SLABS, SUBLANES, LANES), F32)],
        compiler_params=_params(("arbitrary", "arbitrary")),
        name="rec_block",
    )(x3d, g, win, cw, cb, wax, ba, bx, lam, wout)


def _head_norm_rope(t, hn, cos, sin_signed, first_half, ones_blk):
    sq = t * t
    hi = sq.astype(BF16)
    lo = (sq - hi.astype(F32)).astype(BF16)
    ssum = jnp.dot(jnp.concatenate([hi, lo], axis=1), ones_blk, preferred_element_type=F32)
    tn = t * lax.rsqrt(ssum * (1.0 / HEAD_DIM) + EPS) * hn
    partner = jnp.where(first_half, pltpu.roll(tn, LANES - HEAD_DIM // 2, 1),
                        pltpu.roll(tn, HEAD_DIM // 2, 1))
    return tn * cos + partner * sin_signed


def _rope_consts(rows):
    lane = lax.broadcasted_iota(jnp.int32, (rows, LANES), 1)
    first_half = (lane % HEAD_DIM) < (HEAD_DIM // 2)
    r = (lax.broadcasted_iota(jnp.int32, (2 * LANES, LANES), 0) % LANES) // HEAD_DIM
    c = lax.broadcasted_iota(jnp.int32, (2 * LANES, LANES), 1) // HEAD_DIM
    ones_blk = jnp.where(r == c, 1.0, 0.0).astype(BF16)
    return first_half, ones_blk


def _kv_kernel(x_ref, g_ref, wk_ref, wv_ref, hn_ref, cos_ref, sin_ref, k_ref, vt_ref):
    x = x_ref[0]
    h = _rms(x, g_ref[...], EPS).astype(BF16)
    first_half, ones_blk = _rope_consts(SEQ_ROWS)
    cos = cos_ref[...]
    sin_signed = sin_ref[...]
    hn = hn_ref[...]
    for cc in range(D_MODEL // PROJ_COLS):
        cols = slice(cc * PROJ_COLS, (cc + 1) * PROJ_COLS)
        k = jnp.dot(h, wk_ref[:, cols], preferred_element_type=F32)
        v = jnp.dot(h, wv_ref[:, cols], preferred_element_type=F32)
        for i in range(PROJ_COLS // LANES):
            c = cc * (PROJ_COLS // LANES) + i
            sl = slice(i * LANES, (i + 1) * LANES)
            kr = _head_norm_rope(k[:, sl], hn, cos, sin_signed, first_half, ones_blk)
            k_ref[0, c] = kr.astype(BF16)
            vt_ref[0, c, 0, :LANES, :] = v[:, sl].T.astype(BF16)
            pad_row = lax.broadcasted_iota(jnp.int32, (VT_ROWS - LANES, SEQ_ROWS), 0)
            vt_ref[0, c, 0, LANES:, :] = jnp.where(pad_row == 0, 1.0, 0.0).astype(BF16)


def _kv(x3d, g, wk, wv, hn, cos, sin_signed):
    b, s, _ = x3d.shape
    nt = s // SEQ_ROWS
    seq_spec = pl.BlockSpec((1, SEQ_ROWS, D_MODEL), lambda bi, si: (bi, si, 0))
    tab = pl.BlockSpec((SEQ_ROWS, LANES), lambda bi, si: (si, 0))
    return pl.pallas_call(
        _kv_kernel,
        out_shape=(jax.ShapeDtypeStruct((b, N_HEADS, s, LANES), BF16),
                   jax.ShapeDtypeStruct((b, N_HEADS, nt, VT_ROWS, SEQ_ROWS), BF16)),
        grid=(b, nt),
        in_specs=[seq_spec, _const_spec((1, D_MODEL)), _const_spec((D_MODEL, D_MODEL)),
                  _const_spec((D_MODEL, D_MODEL)), _const_spec((1, LANES)), tab, tab],
        out_specs=(pl.BlockSpec((1, N_HEADS, SEQ_ROWS, LANES), lambda bi, si: (bi, 0, si, 0)),
                   pl.BlockSpec((1, N_HEADS, 1, VT_ROWS, SEQ_ROWS),
                                lambda bi, si: (bi, 0, si, 0, 0))),
        compiler_params=_params(("arbitrary", "arbitrary")),
        name="kv_proj",
    )(x3d, g, wk, wv, hn, cos, sin_signed)


def _q_kernel(x_ref, g_ref, wqt_ref, hn_ref, cos_ref, sin_ref, qt_ref):
    x = x_ref[0]
    ht = _rms(x, g_ref[...], EPS).T.astype(BF16)
    cos = cos_ref[...]
    sin_signed = sin_ref[...]
    hn = jnp.tile(hn_ref[...], (1, SEQ_ROWS // LANES))
    half = HEAD_DIM // 2

    def project(cc):
        return jnp.dot(wqt_ref[cc * PROJ_COLS:(cc + 1) * PROJ_COLS, :], ht,
                       preferred_element_type=F32)

    n_chunks = D_MODEL // PROJ_COLS
    q_next = project(0)
    for cc in range(n_chunks):
        q = q_next
        if cc + 1 < n_chunks:
            q_next = project(cc + 1)
        for i in range(PROJ_COLS // LANES):
            c = cc * (PROJ_COLS // LANES) + i
            parts = []
            for comp in range(LANES // HEAD_DIM):
                t = q[i * LANES + comp * HEAD_DIM:i * LANES + (comp + 1) * HEAD_DIM, :]
                ms = jnp.mean(t * t, axis=0, keepdims=True)
                rows = slice(comp * HEAD_DIM, (comp + 1) * HEAD_DIM)
                tn = t * lax.rsqrt(ms + EPS) * hn[rows, :]
                partner = jnp.concatenate([tn[half:], tn[:half]], axis=0)
                parts.append(tn * cos[rows, :] + partner * sin_signed[rows, :])
            qr = jnp.concatenate(parts, axis=0)
            qt_ref[0, c, 0] = (qr * (HEAD_DIM ** -0.5 * LOG2E)).astype(BF16)


def _qproj(x3d, layer, j, g, wqt, hn, cos_t, sin_t):
    b, s, _ = x3d.shape
    nt = s // SEQ_ROWS
    seq_spec = pl.BlockSpec((1, SEQ_ROWS, D_MODEL), lambda bi, si: (bi, si, 0))
    tab = pl.BlockSpec((LANES, SEQ_ROWS), lambda bi, si: (0, si))
    return pl.pallas_call(
        _q_kernel,
        out_shape=jax.ShapeDtypeStruct((b, N_HEADS, nt, LANES, SEQ_ROWS), BF16),
        grid=(b, nt),
        in_specs=[seq_spec, _layer_spec((1, D_MODEL), layer), _layer_spec((D_MODEL, D_MODEL), j),
                  _layer_spec((LANES, LANES), j), tab, tab],
        out_specs=pl.BlockSpec((1, N_HEADS, 1, LANES, SEQ_ROWS), lambda bi, si: (bi, 0, si, 0, 0)),
        compiler_params=_params(("arbitrary", "arbitrary")),
        name="q_proj",
    )(x3d, g, wqt, hn, cos_t, sin_t)


def _attn_kernel(bound_ref, qt_ref, k_ref, vt_ref, lq1_ref, lk1_ref, lq2_ref, lk2_ref, sn_ref,
                 o_ref, s_ref, p_ref, m_ref, acc_ref, *, lambda_init):
    qi = pl.program_id(2)
    bound = bound_ref[0]
    chains = []
    for hh in range(ATT_HEADS):
        qt = qt_ref[0, hh, 0]
        zero = jnp.zeros((HEAD_DIM, ATT_Q), BF16)
        chains += [(hh, jnp.concatenate([qt[:HEAD_DIM], zero], axis=0)),
                   (hh, jnp.concatenate([zero, qt[HEAD_DIM:]], axis=0))]
    n_chain = len(chains)
    n_lane_groups = ATT_Q // LANES

    half_k, half_q = ATT_K // 2, ATT_Q // 2

    def scores(c, j, skip_masked=False):
        hh, qz = chains[c]
        kj = k_ref[0, hh, pl.ds(pl.multiple_of(j * ATT_K, ATT_K), ATT_K), :]
        if skip_masked:
            s_ref[c, :half_k, :] = jnp.dot(kj[:half_k], qz, preferred_element_type=F32)
            s_ref[c, half_k:, half_q:] = jnp.dot(kj[half_k:], qz[:, half_q:],
                                                  preferred_element_type=F32)
        else:
            s_ref[c] = jnp.dot(kj, qz, preferred_element_type=F32)

    def exp_blocks(c, shift, diagonal):
        for rb in range(ATT_K // ATT_ROWS):
            rows = slice(rb * ATT_ROWS, (rb + 1) * ATT_ROWS)
            first_col = ((rb * ATT_ROWS) // CHUNK) * CHUNK if diagonal else 0
            for g in range(n_lane_groups):
                cols = slice(g * LANES, (g + 1) * LANES)
                if (g + 1) * LANES <= first_col:
                    p_ref[c, rows, cols] = jnp.zeros((ATT_ROWS, LANES), BF16)
                    continue
                if jnp.ndim(shift) == 0:
                    sb = shift
                else:
                    sb = jnp.tile(shift[:, cols], (ATT_ROWS // SUBLANES, 1))
                p = jnp.exp2(s_ref[c, rows, cols] - sb)
                if g * LANES < first_col:
                    lane = lax.broadcasted_iota(jnp.int32, (ATT_ROWS, LANES), 1)
                    p = jnp.where(lane >= first_col - g * LANES, p, 0.0)
                p_ref[c, rows, cols] = p.astype(BF16)

    def pv(c, j):
        return jnp.dot(vt_ref[0, chains[c][0], j], p_ref[c], preferred_element_type=F32)

    def bounded_tile(j, diagonal):
        scores(0, j, diagonal)
        scores(1, j, diagonal)
        for c in range(n_chain):
            exp_blocks(c, bound, diagonal)
            if c + 2 < n_chain:
                scores(c + 2, j, diagonal)
            if diagonal:
                vt = vt_ref[0, chains[c][0], j]
                acc_ref[c] = jnp.dot(vt[:, :half_k], p_ref[c, :half_k, :],
                                     preferred_element_type=F32)
                acc_ref[c, :, half_q:] += jnp.dot(vt[:, half_k:], p_ref[c, half_k:, half_q:],
                                                  preferred_element_type=F32)
            else:
                acc_ref[c] += pv(c, j)

    def online_tile(j, diagonal):
        scores(0, j)
        scores(1, j)
        for c in range(n_chain):
            if diagonal:
                k_chunk = lax.broadcasted_iota(jnp.int32, (ATT_K, ATT_Q), 0) // CHUNK
                q_chunk = lax.broadcasted_iota(jnp.int32, (ATT_K, ATT_Q), 1) // CHUNK
                s_all = jnp.where(k_chunk <= q_chunk, s_ref[c], -jnp.inf)
            else:
                s_all = s_ref[c]
            m_old = m_ref[c]
            m_new = jnp.maximum(m_old, jnp.max(s_all, axis=0, keepdims=True))
            alpha = jnp.exp2(m_old - m_new)
            m_ref[c] = m_new
            exp_blocks(c, m_new, diagonal)
            if c + 2 < n_chain:
                scores(c + 2, j)
            acc_ref[c] = jnp.tile(alpha, (VT_ROWS // SUBLANES, 1)) * acc_ref[c] + pv(c, j)

    def attend(tile, diagonal_first):
        def full_tile(j, carry):
            tile(j, False)
            return carry

        if diagonal_first:
            tile(qi, True)
            lax.fori_loop(0, qi, full_tile, 0)
        else:
            acc_ref[...] = jnp.zeros(acc_ref.shape, F32)
            lax.fori_loop(0, qi, full_tile, 0)
            tile(qi, True)

        lam = (jnp.exp(jnp.sum(lq1_ref[...] * lk1_ref[...], axis=-1, keepdims=True))
               - jnp.exp(jnp.sum(lq2_ref[...] * lk2_ref[...], axis=-1, keepdims=True))
               + lambda_init)
        gain = jnp.tile(sn_ref[...], (1, n_lane_groups))
        for hh in range(ATT_HEADS):
            c1, c2 = 2 * hh, 2 * hh + 1
            inv_l1 = 1.0 / acc_ref[c1, LANES:LANES + 1, :]
            inv_l2 = 1.0 / acc_ref[c2, LANES:LANES + 1, :]
            ot = acc_ref[c1, :LANES, :] * inv_l1 - lam * (acc_ref[c2, :LANES, :] * inv_l2)
            ms = jnp.mean(ot * ot, axis=0, keepdims=True)
            yt = ot * lax.rsqrt(ms + SUBLN_EPS) * gain * (1.0 - lambda_init)
            o_ref[0, :, hh * LANES:(hh + 1) * LANES] = yt.T.astype(BF16)

    @pl.when(bound <= STABILIZER_LIMIT)
    def _():
        attend(bounded_tile, diagonal_first=True)

    @pl.when(jnp.logical_not(bound <= STABILIZER_LIMIT))
    def _():
        m_ref[...] = jnp.full(m_ref.shape, -jnp.inf, F32)
        attend(online_tile, diagonal_first=False)


def _attn(qt, k, vt, j, bound, lq1, lk1, lq2, lk2, sn, lambda_init):
    b, nh, s, _ = k.shape
    nt = s // ATT_K
    vec = _const_spec((1, HEAD_DIM))
    lvec = _layer_spec((1, HEAD_DIM), j)
    n_chain = 2 * ATT_HEADS
    return pl.pallas_call(
        functools.partial(_attn_kernel, lambda_init=lambda_init),
        out_shape=jax.ShapeDtypeStruct((b, s, D_MODEL), BF16),
        grid=(b, nh // ATT_HEADS, s // ATT_Q),
        in_specs=[pl.BlockSpec(memory_space=pltpu.SMEM),
                  pl.BlockSpec((1, ATT_HEADS, 1, LANES, ATT_Q),
                               lambda bi, hi, qi: (bi, hi, qi, 0, 0)),
                  pl.BlockSpec((1, ATT_HEADS, s, LANES), lambda bi, hi, qi: (bi, hi, 0, 0)),
                  pl.BlockSpec((1, ATT_HEADS, nt, VT_ROWS, ATT_K),
                               lambda bi, hi, qi: (bi, hi, 0, 0, 0)),
                  lvec, vec, lvec, vec, _layer_spec((LANES, LANES), j)],
        out_specs=pl.BlockSpec((1, ATT_Q, ATT_HEADS * LANES), lambda bi, hi, qi: (bi, qi, hi)),
        scratch_shapes=[pltpu.VMEM((n_chain, ATT_K, ATT_Q), F32),
                        pltpu.VMEM((n_chain, ATT_K, ATT_Q), BF16),
                        pltpu.VMEM((n_chain, SUBLANES, ATT_Q), F32),
                        pltpu.VMEM((n_chain, VT_ROWS, ATT_Q), F32)],
        compiler_params=_params(("arbitrary", "arbitrary", "arbitrary")),
        name="diff_attn",
    )(bound, qt, k, vt, lq1, lk1, lq2, lk2, sn)


def _rope_tables(s):
    pos = jnp.arange(s, dtype=F32)
    inv_freq = ROPE_THETA ** (-jnp.arange(0, HEAD_DIM, 2, dtype=F32) / HEAD_DIM)
    ang = pos[:, None] * inv_freq[None, :]
    ang = jnp.concatenate([ang, ang], axis=-1)
    cos, sin = jnp.cos(ang), jnp.sin(ang)
    half = HEAD_DIM // 2
    sin_signed = jnp.concatenate([-sin[:, :half], sin[:, half:]], axis=-1)
    reps = LANES // HEAD_DIM
    return jnp.tile(cos, (1, reps)), jnp.tile(sin_signed, (1, reps))


def kernel(x, ffn1_norm, ffn1_w_gate, ffn1_w_up, ffn1_w_down, ffn2_norm, ffn2_w_gate, ffn2_w_up, ffn2_w_down, mix_norm, rec_w_in, rec_conv_w, rec_conv_b, rec_w_a, rec_b_a, rec_w_x, rec_b_x, rec_lambda, rec_w_out, kv_norm, w_k, w_v, k_norm, lambda_k1, lambda_k2, attn_w_q, q_norm, lambda_q1, lambda_q2, sub_norm, attn_w_o):
    b, s, d = x.shape
    m = b * s
    reps = LANES // HEAD_DIM
    cos, sin_signed = _rope_tables(s)

    f1 = (ffn1_w_gate, ffn1_w_up, ffn1_w_down)
    f2 = (ffn2_w_gate, ffn2_w_up, ffn2_w_down)
    w_in = rec_w_in.astype(BF16)
    w_ax = jnp.concatenate([rec_w_a, rec_w_x], axis=-1).astype(BF16)
    w_out = rec_w_out.astype(BF16)
    wk, wv = w_k.astype(BF16), w_v.astype(BF16)
    wqt = attn_w_q.astype(BF16).transpose(0, 2, 1)
    wo = attn_w_o.astype(BF16)

    def row(vec):
        return vec.reshape(1, -1)

    def rows(stacked):
        return stacked.reshape(stacked.shape[0], 1, stacked.shape[1])

    g1, g2, gmix = rows(ffn1_norm), rows(ffn2_norm), rows(mix_norm)
    conv_b, b_a, b_x, lam = rows(rec_conv_b), rows(rec_b_a), rows(rec_b_x), rows(rec_lambda)
    q_hn = jnp.broadcast_to(jnp.tile(q_norm, (1, reps))[:, :, None], (q_norm.shape[0], LANES, LANES))
    cos_t, sin_t = cos.T, sin_signed.T
    lq1, lq2 = rows(lambda_q1), rows(lambda_q2)
    sub_g = jnp.broadcast_to(sub_norm[:, :, None], sub_norm.shape + (LANES,))

    k_shared = vt_shared = None
    x2 = x.reshape(m, d)
    for layer in range(DEPTH):
        if layer == N_A_LAYERS:
            k_shared, vt_shared = _kv(x2.reshape(b, s, d), row(kv_norm), wk, wv,
                                      row(jnp.tile(k_norm, reps)), cos, sin_signed)
        x2 = _ffn(x2, layer, g1, *f1)
        if layer < N_A_LAYERS:
            a = layer
            x2 = _rec(x2.reshape(b, s, d), layer, a, gmix, w_in, rec_conv_w, conv_b, w_ax, b_a,
                      b_x, lam, w_out).reshape(m, d)
            x2 = _ffn(x2, layer, g2, *f2)
        else:
            j = layer - N_A_LAYERS
            lambda_init = 0.8 - 0.6 * math.exp(-0.3 * layer)
            qt = _qproj(x2.reshape(b, s, d), layer, j, gmix, wqt, q_hn, cos_t, sin_t)
            bound = (SCORE_BOUND_MARGIN * HEAD_DIM ** 0.5 * LOG2E
                     * jnp.max(jnp.abs(q_norm[j])) * jnp.max(jnp.abs(k_norm))).reshape(1)
            o = _attn(qt, k_shared, vt_shared, j, bound, lq1, row(lambda_k1), lq2, row(lambda_k2),
                      sub_g, lambda_init)
            x2 = _ffn(x2, layer, g2, *f2, attn=(o.reshape(m, d), wo, j))
    return x2.reshape(b, s, d)
```

```python
import functools
import math

import jax
import jax.numpy as jnp
from jax import lax
from jax.experimental import pallas as pl
from jax.experimental.pallas import tpu as pltpu

F32 = jnp.float32
BF16 = jnp.bfloat16

D_MODEL = 1024
D_FF = 2816
DEPTH = 4
N_A_LAYERS = DEPTH // 2
CHUNK = 64
D_RNN = D_MODEL
N_LRU_BLOCKS = 8
LRU_BLOCK = D_RNN // N_LRU_BLOCKS
CONV_WIDTH = 4
LRU_C = 8.0
N_HEADS = 8
HEAD_DIM = 64
ROPE_THETA = 10000.0
EPS = 1e-6
SUBLN_EPS = 1e-5

LANES = 128
SUBLANES = 8
VMEM_LIMIT = 56 * 1024 * 1024

FFN_ROWS = 512
FFN_ATTN_ROWS = 512
FFN_COLS = 256
FFN_HEAD_GROUPS = 2
SEQ_ROWS = 512
REC_SEGS = SUBLANES
REC_SEG_LEN = SEQ_ROWS // REC_SEGS
REC_HALO = CONV_WIDTH - 1
REC_SLABS = D_RNN // LANES
PROJ_COLS = 256
ATT_Q = 512
ATT_K = 512
ATT_HEADS = 4
ATT_ROWS = 64
VT_ROWS = 144
LOG2E = 1.4426950408889634
STABILIZER_LIMIT = 50.0
SCORE_BOUND_MARGIN = 1.02


def _rms(xf, g, eps):
    ms = jnp.mean(xf * xf, axis=-1, keepdims=True)
    return xf * lax.rsqrt(ms + eps) * g


def _params(sem):
    return pltpu.CompilerParams(dimension_semantics=sem, vmem_limit_bytes=VMEM_LIMIT)


def _const_spec(shape):
    nd = len(shape)
    return pl.BlockSpec(shape, lambda *_: (0,) * nd, pipeline_mode=pl.Buffered(1))


def _layer_spec(shape, layer):
    nd = len(shape)
    return pl.BlockSpec((None,) + tuple(shape), lambda *_: (layer,) + (0,) * nd,
                        pipeline_mode=pl.Buffered(1))


def _ffn_kernel(*refs, layer, with_attn):
    if with_attn:
        x_ref, ao_ref, wo_ref, g_ref, wg_hbm, wu_hbm, wd_hbm, o_ref = refs[:8]
    else:
        x_ref, g_ref, wg_hbm, wu_hbm, wd_hbm, o_ref = refs[:6]
    act_ref, wg_ref, wu_ref, wd_ref, stage_g, stage_u, stage_d, sems = refs[-8:]
    step = pl.program_id(0)
    n_chunks = D_FF // FFN_COLS

    def chunk_copies(c, slot):
        cols = pl.ds(c * FFN_COLS, FFN_COLS)
        return (pltpu.make_async_copy(wg_hbm.at[layer, :, cols], stage_g.at[slot], sems.at[slot, 0]),
                pltpu.make_async_copy(wu_hbm.at[layer, :, cols], stage_u.at[slot], sems.at[slot, 1]),
                pltpu.make_async_copy(wd_hbm.at[layer, cols, :], stage_d.at[slot], sems.at[slot, 2]))

    def fetch_and_cast(c):
        slot = c % 2
        sl = slice(c * FFN_COLS, (c + 1) * FFN_COLS)
        if c + 1 < n_chunks:
            for cp in chunk_copies(c + 1, 1 - slot):
                cp.start()
        for cp in chunk_copies(c, slot):
            cp.wait()
        wg_ref[:, sl] = stage_g[slot].astype(BF16)
        wu_ref[:, sl] = stage_u[slot].astype(BF16)
        wd_ref[sl, :] = stage_d[slot].astype(BF16)

    def body(load_weights):
        if load_weights:
            for cp in chunk_copies(0, 0):
                cp.start()
        group_rows = x_ref.shape[0] // FFN_HEAD_GROUPS
        xs, hs = [], []
        for r in range(FFN_HEAD_GROUPS):
            rows = slice(r * group_rows, (r + 1) * group_rows)
            xr = x_ref[rows, :]
            if with_attn:
                xr = xr + jnp.dot(ao_ref[rows, :], wo_ref[...], preferred_element_type=F32)
            xs.append(xr)
            hs.append(_rms(xr, g_ref[...], EPS).astype(BF16))
        if load_weights:
            fetch_and_cast(0)
        first = slice(0, FFN_COLS)
        for r in range(FFN_HEAD_GROUPS):
            rows = slice(r * group_rows, (r + 1) * group_rows)
            gate = jnp.dot(hs[r], wg_ref[:, first], preferred_element_type=F32)
            up = jnp.dot(hs[r], wu_ref[:, first], preferred_element_type=F32)
            act_ref[rows, first] = (gate * jax.nn.sigmoid(gate) * up).astype(BF16)
        x = jnp.concatenate(xs, axis=0)
        h = jnp.concatenate(hs, axis=0)
        for c in range(1, n_chunks):
            if load_weights:
                fetch_and_cast(c)
            sl = slice(c * FFN_COLS, (c + 1) * FFN_COLS)
            gate = jnp.dot(h, wg_ref[:, sl], preferred_element_type=F32)
            up = jnp.dot(h, wu_ref[:, sl], preferred_element_type=F32)
            act_ref[:, sl] = (gate * jax.nn.sigmoid(gate) * up).astype(BF16)
        y = jnp.dot(act_ref[...], wd_ref[...], preferred_element_type=F32)
        o_ref[...] = x + 0.5 * y

    @pl.when(step == 0)
    def _():
        body(True)

    @pl.when(step != 0)
    def _():
        body(False)


def _ffn(x2d, layer, g, wg, wu, wd, attn=None):
    m = x2d.shape[0]
    rows = min(m, FFN_ROWS if attn is None else FFN_ATTN_ROWS)
    row_spec = pl.BlockSpec((rows, D_MODEL), lambda i: (i, 0))
    hbm = pl.BlockSpec(memory_space=pl.ANY)
    operands, in_specs = [x2d], [row_spec]
    if attn is not None:
        o2d, wo, j = attn
        operands += [o2d, wo]
        in_specs += [row_spec, _layer_spec((D_MODEL, D_MODEL), j)]
    operands += [g, wg, wu, wd]
    in_specs += [_layer_spec((1, D_MODEL), layer), hbm, hbm, hbm]
    return pl.pallas_call(
        functools.partial(_ffn_kernel, layer=layer, with_attn=attn is not None),
        out_shape=jax.ShapeDtypeStruct((m, D_MODEL), F32),
        grid=(m // rows,),
        in_specs=in_specs,
        out_specs=row_spec,
        scratch_shapes=[pltpu.VMEM((rows, D_FF), BF16),
                        pltpu.VMEM((D_MODEL, D_FF), BF16),
                        pltpu.VMEM((D_MODEL, D_FF), BF16),
                        pltpu.VMEM((D_FF, D_MODEL), BF16),
                        pltpu.VMEM((2, D_MODEL, FFN_COLS), F32),
                        pltpu.VMEM((2, D_MODEL, FFN_COLS), F32),
                        pltpu.VMEM((2, FFN_COLS, D_MODEL), F32),
                        pltpu.SemaphoreType.DMA((2, 3))],
        compiler_params=_params(("arbitrary",)),
        name="ffn_attn_out" if attn is not None else "ffn",
    )(*operands)


def _rec_kernel(x_ref, g_ref, win_ref, cw_ref, cb_ref, wax_ref, ba_ref, bx_ref, lam_ref,
                wout_ref, o_ref, ext_ref, a_ref, b_ref, tail_ref, carry_ref):
    ts = SEQ_ROWS
    halo_rows = REC_HALO * SUBLANES

    @pl.when(pl.program_id(1) == 0)
    def _():
        tail_ref[...] = jnp.zeros(tail_ref.shape, F32)
        carry_ref[...] = jnp.zeros(carry_ref.shape, F32)

    x = x_ref[0]
    h = _rms(x, g_ref[...], EPS).astype(BF16)
    proj = jnp.dot(h, win_ref[...], preferred_element_type=F32)
    gate = proj[:, :D_RNN]

    lam = lam_ref[...]
    neg = -lam
    softplus = jnp.maximum(neg, 0.0) + jnp.log1p(jnp.exp(-jnp.abs(neg)))
    half_c_softplus = (-0.5 * LRU_C) * softplus
    sub = lax.broadcasted_iota(jnp.int32, (SUBLANES, LANES), 0)

    for n in range(REC_SLABS):
        sl = slice(n * LANES, (n + 1) * LANES)
        for j in range(REC_SEGS):
            ext_ref[n, pl.ds(halo_rows + j, REC_SEG_LEN, stride=REC_SEGS), :] = (
                proj[j * REC_SEG_LEN:(j + 1) * REC_SEG_LEN, D_RNN + n * LANES:D_RNN + (n + 1) * LANES])
        for v in range(REC_HALO):
            rows = slice(SUBLANES * v, SUBLANES * (v + 1))
            cur = ext_ref[n, ts + SUBLANES * v:ts + SUBLANES * (v + 1), :]
            prev = tail_ref[n, rows, :]
            ext_ref[n, rows, :] = jnp.where(sub == 0, pltpu.roll(prev, 1, 0), pltpu.roll(cur, 1, 0))
            tail_ref[n, rows, :] = cur

        conv = cb_ref[:, sl]
        for k in range(CONV_WIDTH):
            conv = conv + ext_ref[n, SUBLANES * k:SUBLANES * k + ts, :] * cw_ref[k:k + 1, sl]

        gax = jnp.dot(conv.astype(BF16), wax_ref[n], preferred_element_type=F32)
        t_a = jnp.tanh(0.5 * (gax[:, :LRU_BLOCK] + ba_ref[:, sl]))
        i = 0.5 * jnp.tanh(0.5 * (gax[:, LRU_BLOCK:] + bx_ref[:, sl])) + 0.5
        log_a = half_c_softplus[:, sl] * t_a + half_c_softplus[:, sl]
        a = jnp.exp(log_a)
        a_ref[n] = a
        one_minus_a2 = -jnp.tanh(log_a) * (a * a + 1.0)
        b_ref[n] = jnp.sqrt(one_minus_a2) * (i * conv)

    hs = [jnp.zeros((SUBLANES, LANES), F32)] * REC_SLABS
    ps = [jnp.ones((SUBLANES, LANES), F32)] * REC_SLABS
    for i in range(REC_SEG_LEN):
        rows = slice(SUBLANES * i, SUBLANES * (i + 1))
        for n in range(REC_SLABS):
            a_i = a_ref[n, rows, :]
            hs[n] = a_i * hs[n] + b_ref[n, rows, :]
            ps[n] = a_i * ps[n]
            b_ref[n, rows, :] = hs[n]
            a_ref[n, rows, :] = ps[n]

    cols = []
    for n in range(REC_SLABS):
        carry_in = carry_ref[n]
        c = carry_in
        for _ in range(REC_SEGS - 1):
            c = jnp.where(sub == 0, carry_in, pltpu.roll(hs[n] + ps[n] * c, 1, 0))
        seg_end = hs[n] + ps[n] * c
        carry_ref[n] = jnp.broadcast_to(seg_end[SUBLANES - 1:SUBLANES, :], (SUBLANES, LANES))
        b_ref[n] = b_ref[n] + a_ref[n] * jnp.tile(c, (REC_SEG_LEN, 1))
        cols.append(jnp.concatenate(
            [b_ref[n, pl.ds(j, REC_SEG_LEN, stride=REC_SEGS), :] for j in range(REC_SEGS)], axis=0))
    h_scan = jnp.concatenate(cols, axis=1)

    y = (jax.nn.gelu(gate, approximate=True) * h_scan).astype(BF16)
    o_ref[0] = x + jnp.dot(y, wout_ref[...], preferred_element_type=F32)


def _rec(x3d, layer, a, g, win, cw, cb, wax, ba, bx, lam, wout):
    b, s, _ = x3d.shape
    seq_spec = pl.BlockSpec((1, SEQ_ROWS, D_MODEL), lambda bi, si: (bi, si, 0))
    vec = _layer_spec((1, D_RNN), a)
    return pl.pallas_call(
        _rec_kernel,
        out_shape=jax.ShapeDtypeStruct((b, s, D_MODEL), F32),
        grid=(b, s // SEQ_ROWS),
        in_specs=[seq_spec, _layer_spec((1, D_MODEL), layer),
                  _layer_spec((D_MODEL, 2 * D_RNN), a),
                  _layer_spec((CONV_WIDTH, D_RNN), a), vec,
                  _layer_spec((N_LRU_BLOCKS, LRU_BLOCK, 2 * LRU_BLOCK), a), vec, vec, vec,
                  _layer_spec((D_RNN, D_MODEL), a)],
        out_specs=seq_spec,
        scratch_shapes=[pltpu.VMEM((REC_SLABS, SEQ_ROWS + REC_HALO * SUBLANES, LANES), F32),
                        pltpu.VMEM((REC_SLABS, SEQ_ROWS, LANES), F32),
                        pltpu.VMEM((REC_SLABS, SEQ_ROWS, LANES), F32),
                        pltpu.VMEM((REC_SLABS, REC_HALO * SUBLANES, LANES), F32),
                        pltpu.VMEM((REC_SLABS, SUBLANES, LANES), F32)],
        compiler_params=_params(("arbitrary", "arbitrary")),
        name="rec_block",
    )(x3d, g, win, cw, cb, wax, ba, bx, lam, wout)


def _head_norm_rope(t, hn, cos, sin_signed, first_half, ones_blk):
    sq = t * t
    hi = sq.astype(BF16)
    lo = (sq - hi.astype(F32)).astype(BF16)
    ssum = jnp.dot(jnp.concatenate([hi, lo], axis=1), ones_blk, preferred_element_type=F32)
    tn = t * lax.rsqrt(ssum * (1.0 / HEAD_DIM) + EPS) * hn
    partner = jnp.where(first_half, pltpu.roll(tn, LANES - HEAD_DIM // 2, 1),
                        pltpu.roll(tn, HEAD_DIM // 2, 1))
    return tn * cos + partner * sin_signed


def _rope_consts(rows):
    lane = lax.broadcasted_iota(jnp.int32, (rows, LANES), 1)
    first_half = (lane % HEAD_DIM) < (HEAD_DIM // 2)
    r = (lax.broadcasted_iota(jnp.int32, (2 * LANES, LANES), 0) % LANES) // HEAD_DIM
    c = lax.broadcasted_iota(jnp.int32, (2 * LANES, LANES), 1) // HEAD_DIM
    ones_blk = jnp.where(r == c, 1.0, 0.0).astype(BF16)
    return first_half, ones_blk


def _kv_kernel(x_ref, g_ref, wk_ref, wv_ref, hn_ref, cos_ref, sin_ref, k_ref, vt_ref):
    x = x_ref[0]
    h = _rms(x, g_ref[...], EPS).astype(BF16)
    first_half, ones_blk = _rope_consts(SEQ_ROWS)
    cos = cos_ref[...]
    sin_signed = sin_ref[...]
    hn = hn_ref[...]
    for cc in range(D_MODEL // PROJ_COLS):
        cols = slice(cc * PROJ_COLS, (cc + 1) * PROJ_COLS)
        k = jnp.dot(h, wk_ref[:, cols], preferred_element_type=F32)
        v = jnp.dot(h, wv_ref[:, cols], preferred_element_type=F32)
        for i in range(PROJ_COLS // LANES):
            c = cc * (PROJ_COLS // LANES) + i
            sl = slice(i * LANES, (i + 1) * LANES)
            kr = _head_norm_rope(k[:, sl], hn, cos, sin_signed, first_half, ones_blk)
            k_ref[0, c] = kr.astype(BF16)
            vt_ref[0, c, 0, :LANES, :] = v[:, sl].astype(BF16).T
            pad_row = lax.broadcasted_iota(jnp.int32, (VT_ROWS - LANES, SEQ_ROWS), 0)
            vt_ref[0, c, 0, LANES:, :] = jnp.where(pad_row == 0, 1.0, 0.0).astype(BF16)


def _kv(x3d, g, wk, wv, hn, cos, sin_signed):
    b, s, _ = x3d.shape
    nt = s // SEQ_ROWS
    seq_spec = pl.BlockSpec((1, SEQ_ROWS, D_MODEL), lambda bi, si: (bi, si, 0))
    tab = pl.BlockSpec((SEQ_ROWS, LANES), lambda bi, si: (si, 0))
    return pl.pallas_call(
        _kv_kernel,
        out_shape=(jax.ShapeDtypeStruct((b, N_HEADS, s, LANES), BF16),
                   jax.ShapeDtypeStruct((b, N_HEADS, nt, VT_ROWS, SEQ_ROWS), BF16)),
        grid=(b, nt),
        in_specs=[seq_spec, _const_spec((1, D_MODEL)), _const_spec((D_MODEL, D_MODEL)),
                  _const_spec((D_MODEL, D_MODEL)), _const_spec((1, LANES)), tab, tab],
        out_specs=(pl.BlockSpec((1, N_HEADS, SEQ_ROWS, LANES), lambda bi, si: (bi, 0, si, 0)),
                   pl.BlockSpec((1, N_HEADS, 1, VT_ROWS, SEQ_ROWS),
                                lambda bi, si: (bi, 0, si, 0, 0))),
        compiler_params=_params(("arbitrary", "arbitrary")),
        name="kv_proj",
    )(x3d, g, wk, wv, hn, cos, sin_signed)


def _q_kernel(x_ref, g_ref, wqt_ref, hn_ref, cos_ref, sin_ref, qt_ref):
    x = x_ref[0]
    ht = _rms(x, g_ref[...], EPS).astype(BF16).T
    cos = cos_ref[...]
    sin_signed = sin_ref[...]
    hn = jnp.tile(hn_ref[...], (1, SEQ_ROWS // LANES))
    half = HEAD_DIM // 2

    def project(cc):
        return jnp.dot(wqt_ref[cc * PROJ_COLS:(cc + 1) * PROJ_COLS, :], ht,
                       preferred_element_type=F32)

    n_chunks = D_MODEL // PROJ_COLS
    q_next = project(0)
    for cc in range(n_chunks):
        q = q_next
        if cc + 1 < n_chunks:
            q_next = project(cc + 1)
        for i in range(PROJ_COLS // LANES):
            c = cc * (PROJ_COLS // LANES) + i
            parts = []
            for comp in range(LANES // HEAD_DIM):
                t = q[i * LANES + comp * HEAD_DIM:i * LANES + (comp + 1) * HEAD_DIM, :]
                ms = jnp.mean(t * t, axis=0, keepdims=True)
                rows = slice(comp * HEAD_DIM, (comp + 1) * HEAD_DIM)
                tn = t * lax.rsqrt(ms + EPS) * hn[rows, :]
                partner = jnp.concatenate([tn[half:], tn[:half]], axis=0)
                parts.append(tn * cos[rows, :] + partner * sin_signed[rows, :])
            qr = jnp.concatenate(parts, axis=0)
            qt_ref[0, c, 0] = (qr * (HEAD_DIM ** -0.5 * LOG2E)).astype(BF16)


def _qproj(x3d, layer, j, g, wqt, hn, cos_t, sin_t):
    b, s, _ = x3d.shape
    nt = s // SEQ_ROWS
    seq_spec = pl.BlockSpec((1, SEQ_ROWS, D_MODEL), lambda bi, si: (bi, si, 0))
    tab = pl.BlockSpec((LANES, SEQ_ROWS), lambda bi, si: (0, si))
    return pl.pallas_call(
        _q_kernel,
        out_shape=jax.ShapeDtypeStruct((b, N_HEADS, nt, LANES, SEQ_ROWS), BF16),
        grid=(b, nt),
        in_specs=[seq_spec, _layer_spec((1, D_MODEL), layer), _layer_spec((D_MODEL, D_MODEL), j),
                  _layer_spec((LANES, LANES), j), tab, tab],
        out_specs=pl.BlockSpec((1, N_HEADS, 1, LANES, SEQ_ROWS), lambda bi, si: (bi, 0, si, 0, 0)),
        compiler_params=_params(("arbitrary", "arbitrary")),
        name="q_proj",
    )(x3d, g, wqt, hn, cos_t, sin_t)


def _attn_kernel(bound_ref, qt_ref, k_ref, vt_ref, lq1_ref, lk1_ref, lq2_ref, lk2_ref, sn_ref,
                 o_ref, s_ref, p_ref, m_ref, acc_ref, *, lambda_init):
    qi = pl.program_id(2)
    bound = bound_ref[0]
    chains = []
    for hh in range(ATT_HEADS):
        qt = qt_ref[0, hh, 0]
        zero = jnp.zeros((HEAD_DIM, ATT_Q), BF16)
        chains += [(hh, jnp.concatenate([qt[:HEAD_DIM], zero], axis=0)),
                   (hh, jnp.concatenate([zero, qt[HEAD_DIM:]], axis=0))]
    n_chain = len(chains)
    n_lane_groups = ATT_Q // LANES

    half_k, half_q = ATT_K // 2, ATT_Q // 2

    def scores(c, j, skip_masked=False):
        hh, qz = chains[c]
        kj = k_ref[0, hh, pl.ds(pl.multiple_of(j * ATT_K, ATT_K), ATT_K), :]
        if skip_masked:
            s_ref[c, :half_k, :] = jnp.dot(kj[:half_k], qz, preferred_element_type=F32)
            s_ref[c, half_k:, half_q:] = jnp.dot(kj[half_k:], qz[:, half_q:],
                                                  preferred_element_type=F32)
        else:
            s_ref[c] = jnp.dot(kj, qz, preferred_element_type=F32)

    def exp_blocks(c, shift, diagonal):
        for rb in range(ATT_K // ATT_ROWS):
            rows = slice(rb * ATT_ROWS, (rb + 1) * ATT_ROWS)
            first_col = ((rb * ATT_ROWS) // CHUNK) * CHUNK if diagonal else 0
            for g in range(n_lane_groups):
                cols = slice(g * LANES, (g + 1) * LANES)
                if (g + 1) * LANES <= first_col:
                    p_ref[c, rows, cols] = jnp.zeros((ATT_ROWS, LANES), BF16)
                    continue
                if jnp.ndim(shift) == 0:
                    sb = shift
                else:
                    sb = jnp.tile(shift[:, cols], (ATT_ROWS // SUBLANES, 1))
                p = jnp.exp2(s_ref[c, rows, cols] - sb)
                if g * LANES < first_col:
                    lane = lax.broadcasted_iota(jnp.int32, (ATT_ROWS, LANES), 1)
                    p = jnp.where(lane >= first_col - g * LANES, p, 0.0)
                p_ref[c, rows, cols] = p.astype(BF16)

    def pv(c, j):
        return jnp.dot(vt_ref[0, chains[c][0], j], p_ref[c], preferred_element_type=F32)

    def bounded_tile(j, diagonal):
        scores(0, j, diagonal)
        scores(1, j, diagonal)
        for c in range(n_chain):
            exp_blocks(c, bound, diagonal)
            if c + 2 < n_chain:
                scores(c + 2, j, diagonal)
            if diagonal:
                vt = vt_ref[0, chains[c][0], j]
                acc_ref[c] = jnp.dot(vt[:, :half_k], p_ref[c, :half_k, :],
                                     preferred_element_type=F32)
                acc_ref[c, :, half_q:] += jnp.dot(vt[:, half_k:], p_ref[c, half_k:, half_q:],
                                                  preferred_element_type=F32)
            else:
                acc_ref[c] += pv(c, j)

    def online_tile(j, diagonal):
        scores(0, j)
        scores(1, j)
        for c in range(n_chain):
            if diagonal:
                k_chunk = lax.broadcasted_iota(jnp.int32, (ATT_K, ATT_Q), 0) // CHUNK
                q_chunk = lax.broadcasted_iota(jnp.int32, (ATT_K, ATT_Q), 1) // CHUNK
                s_all = jnp.where(k_chunk <= q_chunk, s_ref[c], -jnp.inf)
            else:
                s_all = s_ref[c]
            m_old = m_ref[c]
            m_new = jnp.maximum(m_old, jnp.max(s_all, axis=0, keepdims=True))
            alpha = jnp.exp2(m_old - m_new)
            m_ref[c] = m_new
            exp_blocks(c, m_new, diagonal)
            if c + 2 < n_chain:
                scores(c + 2, j)
            acc_ref[c] = jnp.tile(alpha, (VT_ROWS // SUBLANES, 1)) * acc_ref[c] + pv(c, j)

    def attend(tile, diagonal_first):
        def full_tile(j, carry):
            tile(j, False)
            return carry

        if diagonal_first:
            tile(qi, True)
            lax.fori_loop(0, qi, full_tile, 0)
        else:
            acc_ref[...] = jnp.zeros(acc_ref.shape, F32)
            lax.fori_loop(0, qi, full_tile, 0)
            tile(qi, True)

        lam = (jnp.exp(jnp.sum(lq1_ref[...] * lk1_ref[...], axis=-1, keepdims=True))
               - jnp.exp(jnp.sum(lq2_ref[...] * lk2_ref[...], axis=-1, keepdims=True))
               + lambda_init)
        gain = jnp.tile(sn_ref[...], (1, n_lane_groups))
        for hh in range(ATT_HEADS):
            c1, c2 = 2 * hh, 2 * hh + 1
            inv_l1 = 1.0 / acc_ref[c1, LANES:LANES + 1, :]
            inv_l2 = 1.0 / acc_ref[c2, LANES:LANES + 1, :]
            ot = acc_ref[c1, :LANES, :] * inv_l1 - lam * (acc_ref[c2, :LANES, :] * inv_l2)
            ms = jnp.mean(ot * ot, axis=0, keepdims=True)
            yt = ot * lax.rsqrt(ms + SUBLN_EPS) * gain * (1.0 - lambda_init)
            o_ref[0, :, hh * LANES:(hh + 1) * LANES] = yt.astype(BF16).T

    @pl.when(bound <= STABILIZER_LIMIT)
    def _():
        attend(bounded_tile, diagonal_first=True)

    @pl.when(jnp.logical_not(bound <= STABILIZER_LIMIT))
    def _():
        m_ref[...] = jnp.full(m_ref.shape, -jnp.inf, F32)
        attend(online_tile, diagonal_first=False)


def _attn(qt, k, vt, j, bound, lq1, lk1, lq2, lk2, sn, lambda_init):
    b, nh, s, _ = k.shape
    nt = s // ATT_K
    vec = _const_spec((1, HEAD_DIM))
    lvec = _layer_spec((1, HEAD_DIM), j)
    n_chain = 2 * ATT_HEADS
    return pl.pallas_call(
        functools.partial(_attn_kernel, lambda_init=lambda_init),
        out_shape=jax.ShapeDtypeStruct((b, s, D_MODEL), BF16),
        grid=(b, nh // ATT_HEADS, s // ATT_Q),
        in_specs=[pl.BlockSpec(memory_space=pltpu.SMEM),
                  pl.BlockSpec((1, ATT_HEADS, 1, LANES, ATT_Q),
                               lambda bi, hi, qi: (bi, hi, qi, 0, 0)),
                  pl.BlockSpec((1, ATT_HEADS, s, LANES), lambda bi, hi, qi: (bi, hi, 0, 0)),
                  pl.BlockSpec((1, ATT_HEADS, nt, VT_ROWS, ATT_K),
                               lambda bi, hi, qi: (bi, hi, 0, 0, 0)),
                  lvec, vec, lvec, vec, _layer_spec((LANES, LANES), j)],
        out_specs=pl.BlockSpec((1, ATT_Q, ATT_HEADS * LANES), lambda bi, hi, qi: (bi, qi, hi)),
        scratch_shapes=[pltpu.VMEM((n_chain, ATT_K, ATT_Q), F32),
                        pltpu.VMEM((n_chain, ATT_K, ATT_Q), BF16),
                        pltpu.VMEM((n_chain, SUBLANES, ATT_Q), F32),
                        pltpu.VMEM((n_chain, VT_ROWS, ATT_Q), F32)],
        compiler_params=_params(("arbitrary", "arbitrary", "arbitrary")),
        name="diff_attn",
    )(bound, qt, k, vt, lq1, lk1, lq2, lk2, sn)


def _rope_tables(s):
    pos = jnp.arange(s, dtype=F32)
    inv_freq = ROPE_THETA ** (-jnp.arange(0, HEAD_DIM, 2, dtype=F32) / HEAD_DIM)
    ang = pos[:, None] * inv_freq[None, :]
    ang = jnp.concatenate([ang, ang], axis=-1)
    cos, sin = jnp.cos(ang), jnp.sin(ang)
    half = HEAD_DIM // 2
    sin_signed = jnp.concatenate([-sin[:, :half], sin[:, half:]], axis=-1)
    reps = LANES // HEAD_DIM
    return jnp.tile(cos, (1, reps)), jnp.tile(sin_signed, (1, reps))


def kernel(x, ffn1_norm, ffn1_w_gate, ffn1_w_up, ffn1_w_down, ffn2_norm, ffn2_w_gate, ffn2_w_up, ffn2_w_down, mix_norm, rec_w_in, rec_conv_w, rec_conv_b, rec_w_a, rec_b_a, rec_w_x, rec_b_x, rec_lambda, rec_w_out, kv_norm, w_k, w_v, k_norm, lambda_k1, lambda_k2, attn_w_q, q_norm, lambda_q1, lambda_q2, sub_norm, attn_w_o):
    b, s, d = x.shape
    m = b * s
    reps = LANES // HEAD_DIM
    cos, sin_signed = _rope_tables(s)

    f1 = (ffn1_w_gate, ffn1_w_up, ffn1_w_down)
    f2 = (ffn2_w_gate, ffn2_w_up, ffn2_w_down)
    w_in = rec_w_in.astype(BF16)
    w_ax = jnp.concatenate([rec_w_a, rec_w_x], axis=-1).astype(BF16)
    w_out = rec_w_out.astype(BF16)
    wk, wv = w_k.astype(BF16), w_v.astype(BF16)
    wqt = attn_w_q.astype(BF16).transpose(0, 2, 1)
    wo = attn_w_o.astype(BF16)

    def row(vec):
        return vec.reshape(1, -1)

    def rows(stacked):
        return stacked.reshape(stacked.shape[0], 1, stacked.shape[1])

    g1, g2, gmix = rows(ffn1_norm), rows(ffn2_norm), rows(mix_norm)
    conv_b, b_a, b_x, lam = rows(rec_conv_b), rows(rec_b_a), rows(rec_b_x), rows(rec_lambda)
    q_hn = jnp.broadcast_to(jnp.tile(q_norm, (1, reps))[:, :, None], (q_norm.shape[0], LANES, LANES))
    cos_t, sin_t = cos.T, sin_signed.T
    lq1, lq2 = rows(lambda_q1), rows(lambda_q2)
    sub_g = jnp.broadcast_to(sub_norm[:, :, None], sub_norm.shape + (LANES,))

    k_shared = vt_shared = None
    x2 = x.reshape(m, d)
    for layer in range(DEPTH):
        if layer == N_A_LAYERS:
            k_shared, vt_shared = _kv(x2.reshape(b, s, d), row(kv_norm), wk, wv,
                                      row(jnp.tile(k_norm, reps)), cos, sin_signed)
        x2 = _ffn(x2, layer, g1, *f1)
        if layer < N_A_LAYERS:
            a = layer
            x2 = _rec(x2.reshape(b, s, d), layer, a, gmix, w_in, rec_conv_w, conv_b, w_ax, b_a,
                      b_x, lam, w_out).reshape(m, d)
            x2 = _ffn(x2, layer, g2, *f2)
        else:
            j = layer - N_A_LAYERS
            lambda_init = 0.8 - 0.6 * math.exp(-0.3 * layer)
            qt = _qproj(x2.reshape(b, s, d), layer, j, gmix, wqt, q_hn, cos_t, sin_t)
            bound = (SCORE_BOUND_MARGIN * HEAD_DIM ** 0.5 * LOG2E
                     * jnp.max(jnp.abs(q_norm[j])) * jnp.max(jnp.abs(k_norm))).reshape(1)
            o = _attn(qt, k_shared, vt_shared, j, bound, lq1, row(lambda_k1), lq2, row(lambda_k2),
                      sub_g, lambda_init)
            x2 = _ffn(x2, layer, g2, *f2, attn=(o.reshape(m, d), wo, j))
    return x2.reshape(b, s, d)
```

```python
import functools
import math

import jax
import jax.numpy as jnp
from jax import lax
from jax.experimental import pallas as pl
from jax.experimental.pallas import tpu as pltpu

F32 = jnp.float32
BF16 = jnp.bfloat16

D_MODEL = 1024
D_FF = 2816
DEPTH = 4
N_A_LAYERS = DEPTH // 2
CHUNK = 64
D_RNN = D_MODEL
N_LRU_BLOCKS = 8
LRU_BLOCK = D_RNN // N_LRU_BLOCKS
CONV_WIDTH = 4
LRU_C = 8.0
N_HEADS = 8
HEAD_DIM = 64
ROPE_THETA = 10000.0
EPS = 1e-6
SUBLN_EPS = 1e-5

LANES = 128
SUBLANES = 8
VMEM_LIMIT = 56 * 1024 * 1024

FFN_ROWS = 512
FFN_ATTN_ROWS = 512
FFN_COLS = 256
FFN_HEAD_GROUPS = 2
SEQ_ROWS = 512
REC_SEGS = SUBLANES
REC_SEG_LEN = SEQ_ROWS // REC_SEGS
REC_HALO = CONV_WIDTH - 1
REC_SLABS = D_RNN // LANES
PROJ_COLS = 256
ATT_Q = 512
ATT_K = 512
ATT_HEADS = 4
ATT_ROWS = 64
VT_ROWS = 144
LOG2E = 1.4426950408889634
STABILIZER_LIMIT = 50.0
SCORE_BOUND_MARGIN = 1.02


def _rms(xf, g, eps):
    ms = jnp.mean(xf * xf, axis=-1, keepdims=True)
    return xf * lax.rsqrt(ms + eps) * g


def _params(sem):
    return pltpu.CompilerParams(dimension_semantics=sem, vmem_limit_bytes=VMEM_LIMIT)


def _const_spec(shape):
    nd = len(shape)
    return pl.BlockSpec(shape, lambda *_: (0,) * nd, pipeline_mode=pl.Buffered(1))


def _layer_spec(shape, layer):
    nd = len(shape)
    return pl.BlockSpec((None,) + tuple(shape), lambda *_: (layer,) + (0,) * nd,
                        pipeline_mode=pl.Buffered(1))


def _ffn_kernel(*refs, layer, with_attn):
    if with_attn:
        x_ref, ao_ref, wo_ref, g_ref, wg_hbm, wu_hbm, wd_hbm, o_ref = refs[:8]
    else:
        x_ref, g_ref, wg_hbm, wu_hbm, wd_hbm, o_ref = refs[:6]
    act_ref, wg_ref, wu_ref, wd_ref, stage_g, stage_u, stage_d, sems = refs[-8:]
    step = pl.program_id(0)
    n_chunks = D_FF // FFN_COLS

    def chunk_copies(c, slot):
        cols = pl.ds(c * FFN_COLS, FFN_COLS)
        return (pltpu.make_async_copy(wg_hbm.at[layer, :, cols], stage_g.at[slot], sems.at[slot, 0]),
                pltpu.make_async_copy(wu_hbm.at[layer, :, cols], stage_u.at[slot], sems.at[slot, 1]),
                pltpu.make_async_copy(wd_hbm.at[layer, cols, :], stage_d.at[slot], sems.at[slot, 2]))

    def fetch_and_cast(c):
        slot = c % 2
        sl = slice(c * FFN_COLS, (c + 1) * FFN_COLS)
        if c + 1 < n_chunks:
            for cp in chunk_copies(c + 1, 1 - slot):
                cp.start()
        for cp in chunk_copies(c, slot):
            cp.wait()
        wg_ref[:, sl] = stage_g[slot].astype(BF16)
        wu_ref[:, sl] = stage_u[slot].astype(BF16)
        wd_ref[sl, :] = stage_d[slot].astype(BF16)

    def body(load_weights):
        if load_weights:
            for cp in chunk_copies(0, 0):
                cp.start()
        group_rows = x_ref.shape[0] // FFN_HEAD_GROUPS
        xs, hs = [], []
        for r in range(FFN_HEAD_GROUPS):
            rows = slice(r * group_rows, (r + 1) * group_rows)
            xr = x_ref[rows, :]
            if with_attn:
                xr = xr + jnp.dot(ao_ref[rows, :], wo_ref[...], preferred_element_type=F32)
            xs.append(xr)
            hs.append(_rms(xr, g_ref[...], EPS).astype(BF16))
        if load_weights:
            fetch_and_cast(0)
        first = slice(0, FFN_COLS)
        for r in range(FFN_HEAD_GROUPS):
            rows = slice(r * group_rows, (r + 1) * group_rows)
            gate = jnp.dot(hs[r], wg_ref[:, first], preferred_element_type=F32)
            up = jnp.dot(hs[r], wu_ref[:, first], preferred_element_type=F32)
            act_ref[rows, first] = (gate * jax.nn.sigmoid(gate) * up).astype(BF16)
        x = jnp.concatenate(xs, axis=0)
        h = jnp.concatenate(hs, axis=0)
        for c in range(1, n_chunks):
            if load_weights:
                fetch_and_cast(c)
            sl = slice(c * FFN_COLS, (c + 1) * FFN_COLS)
            gate = jnp.dot(h, wg_ref[:, sl], preferred_element_type=F32)
            up = jnp.dot(h, wu_ref[:, sl], preferred_element_type=F32)
            act_ref[:, sl] = (gate * jax.nn.sigmoid(gate) * up).astype(BF16)
        y = jnp.dot(act_ref[...], wd_ref[...], preferred_element_type=F32)
        o_ref[...] = x + 0.5 * y

    @pl.when(step == 0)
    def _():
        body(True)

    @pl.when(step != 0)
    def _():
        body(False)


def _ffn(x2d, layer, g, wg, wu, wd, attn=None):
    m = x2d.shape[0]
    rows = min(m, FFN_ROWS if attn is None else FFN_ATTN_ROWS)
    row_spec = pl.BlockSpec((rows, D_MODEL), lambda i: (i, 0))
    hbm = pl.BlockSpec(memory_space=pl.ANY)
    operands, in_specs = [x2d], [row_spec]
    if attn is not None:
        o2d, wo, j = attn
        operands += [o2d, wo]
        in_specs += [row_spec, _layer_spec((D_MODEL, D_MODEL), j)]
    operands += [g, wg, wu, wd]
    in_specs += [_layer_spec((1, D_MODEL), layer), hbm, hbm, hbm]
    return pl.pallas_call(
        functools.partial(_ffn_kernel, layer=layer, with_attn=attn is not None),
        out_shape=jax.ShapeDtypeStruct((m, D_MODEL), F32),
        grid=(m // rows,),
        in_specs=in_specs,
        out_specs=row_spec,
        scratch_shapes=[pltpu.VMEM((rows, D_FF), BF16),
                        pltpu.VMEM((D_MODEL, D_FF), BF16),
                        pltpu.VMEM((D_MODEL, D_FF), BF16),
                        pltpu.VMEM((D_FF, D_MODEL), BF16),
                        pltpu.VMEM((2, D_MODEL, FFN_COLS), F32),
                        pltpu.VMEM((2, D_MODEL, FFN_COLS), F32),
                        pltpu.VMEM((2, FFN_COLS, D_MODEL), F32),
                        pltpu.SemaphoreType.DMA((2, 3))],
        compiler_params=_params(("arbitrary",)),
        name="ffn_attn_out" if attn is not None else "ffn",
    )(*operands)


def _rec_kernel(x_ref, g_ref, win_ref, cw_ref, cb_ref, wax_ref, ba_ref, bx_ref, lam_ref,
                wout_ref, o_ref, ext_ref, a_ref, b_ref, tail_ref, carry_ref):
    ts = SEQ_ROWS
    halo_rows = REC_HALO * SUBLANES

    @pl.when(pl.program_id(1) == 0)
    def _():
        tail_ref[...] = jnp.zeros(tail_ref.shape, F32)
        carry_ref[...] = jnp.zeros(carry_ref.shape, F32)

    x = x_ref[0]
    h = _rms(x, g_ref[...], EPS).astype(BF16)
    proj = jnp.dot(h, win_ref[...], preferred_element_type=F32)
    gate = proj[:, :D_RNN]

    lam = lam_ref[...]
    neg = -lam
    softplus = jnp.maximum(neg, 0.0) + jnp.log1p(jnp.exp(-jnp.abs(neg)))
    half_c_softplus = (-0.5 * LRU_C) * softplus
    sub = lax.broadcasted_iota(jnp.int32, (SUBLANES, LANES), 0)

    for n in range(REC_SLABS):
        sl = slice(n * LANES, (n + 1) * LANES)
        for j in range(REC_SEGS):
            ext_ref[n, pl.ds(halo_rows + j, REC_SEG_LEN, stride=REC_SEGS), :] = (
                proj[j * REC_SEG_LEN:(j + 1) * REC_SEG_LEN, D_RNN + n * LANES:D_RNN + (n + 1) * LANES])
        for v in range(REC_HALO):
            rows = slice(SUBLANES * v, SUBLANES * (v + 1))
            cur = ext_ref[n, ts + SUBLANES * v:ts + SUBLANES * (v + 1), :]
            prev = tail_ref[n, rows, :]
            ext_ref[n, rows, :] = jnp.where(sub == 0, pltpu.roll(prev, 1, 0), pltpu.roll(cur, 1, 0))
            tail_ref[n, rows, :] = cur

        conv = cb_ref[:, sl]
        for k in range(CONV_WIDTH):
            conv = conv + ext_ref[n, SUBLANES * k:SUBLANES * k + ts, :] * cw_ref[k:k + 1, sl]

        gax = jnp.dot(conv.astype(BF16), wax_ref[n], preferred_element_type=F32)
        t_a = jnp.tanh(0.5 * (gax[:, :LRU_BLOCK] + ba_ref[:, sl]))
        i = 0.5 * jnp.tanh(0.5 * (gax[:, LRU_BLOCK:] + bx_ref[:, sl])) + 0.5
        log_a = half_c_softplus[:, sl] * t_a + half_c_softplus[:, sl]
        a = jnp.exp(log_a)
        a_ref[n] = a
        one_minus_a2 = -jnp.tanh(log_a) * (a * a + 1.0)
        b_ref[n] = jnp.sqrt(one_minus_a2) * (i * conv)

    hs = [jnp.zeros((SUBLANES, LANES), F32)] * REC_SLABS
    ps = [jnp.ones((SUBLANES, LANES), F32)] * REC_SLABS
    for i in range(REC_SEG_LEN):
        rows = slice(SUBLANES * i, SUBLANES * (i + 1))
        for n in range(REC_SLABS):
            a_i = a_ref[n, rows, :]
            hs[n] = a_i * hs[n] + b_ref[n, rows, :]
            ps[n] = a_i * ps[n]
            b_ref[n, rows, :] = hs[n]
            a_ref[n, rows, :] = ps[n]

    cols = []
    for n in range(REC_SLABS):
        carry_in = carry_ref[n]
        c = carry_in
        for _ in range(REC_SEGS - 1):
            c = jnp.where(sub == 0, carry_in, pltpu.roll(hs[n] + ps[n] * c, 1, 0))
        seg_end = hs[n] + ps[n] * c
        carry_ref[n] = jnp.broadcast_to(seg_end[SUBLANES - 1:SUBLANES, :], (SUBLANES, LANES))
        b_ref[n] = b_ref[n] + a_ref[n] * jnp.tile(c, (REC_SEG_LEN, 1))
        cols.append(jnp.concatenate(
            [b_ref[n, pl.ds(j, REC_SEG_LEN, stride=REC_SEGS), :] for j in range(REC_SEGS)], axis=0))
    h_scan = jnp.concatenate(cols, axis=1)

    y = (jax.nn.gelu(gate, approximate=True) * h_scan).astype(BF16)
    o_ref[0] = x + jnp.dot(y, wout_ref[...], preferred_element_type=F32)


def _rec(x3d, layer, a, g, win, cw, cb, wax, ba, bx, lam, wout):
    b, s, _ = x3d.shape
    seq_spec = pl.BlockSpec((1, SEQ_ROWS, D_MODEL), lambda bi, si: (bi, si, 0))
    vec = _layer_spec((1, D_RNN), a)
    return pl.pallas_call(
        _rec_kernel,
        out_shape=jax.ShapeDtypeStruct((b, s, D_MODEL), F32),
        grid=(b, s // SEQ_ROWS),
        in_specs=[seq_spec, _layer_spec((1, D_MODEL), layer),
                  _layer_spec((D_MODEL, 2 * D_RNN), a),
                  _layer_spec((CONV_WIDTH, D_RNN), a), vec,
                  _layer_spec((N_LRU_BLOCKS, LRU_BLOCK, 2 * LRU_BLOCK), a), vec, vec, vec,
                  _layer_spec((D_RNN, D_MODEL), a)],
        out_specs=seq_spec,
        scratch_shapes=[pltpu.VMEM((REC_SLABS, SEQ_ROWS + REC_HALO * SUBLANES, LANES), F32),
                        pltpu.VMEM((REC_SLABS, SEQ_ROWS, LANES), F32),
                        pltpu.VMEM((REC_SLABS, SEQ_ROWS, LANES), F32),
                        pltpu.VMEM((REC_SLABS, REC_HALO * SUBLANES, LANES), F32),
                        pltpu.VMEM((REC_SLABS, SUBLANES, LANES), F32)],
        compiler_params=_params(("arbitrary", "arbitrary")),
        name="rec_block",
    )(x3d, g, win, cw, cb, wax, ba, bx, lam, wout)


def _head_norm_rope(t, hn, cos, sin_signed, first_half, ones_blk):
    sq = t * t
    hi = sq.astype(BF16)
    lo = (sq - hi.astype(F32)).astype(BF16)
    ssum = jnp.dot(jnp.concatenate([hi, lo], axis=1), ones_blk, preferred_element_type=F32)
    tn = t * lax.rsqrt(ssum * (1.0 / HEAD_DIM) + EPS) * hn
    partner = jnp.where(first_half, pltpu.roll(tn, LANES - HEAD_DIM // 2, 1),
                        pltpu.roll(tn, HEAD_DIM // 2, 1))
    return tn * cos + partner * sin_signed


def _rope_consts(rows):
    lane = lax.broadcasted_iota(jnp.int32, (rows, LANES), 1)
    first_half = (lane % HEAD_DIM) < (HEAD_DIM // 2)
    r = (lax.broadcasted_iota(jnp.int32, (2 * LANES, LANES), 0) % LANES) // HEAD_DIM
    c = lax.broadcasted_iota(jnp.int32, (2 * LANES, LANES), 1) // HEAD_DIM
    ones_blk = jnp.where(r == c, 1.0, 0.0).astype(BF16)
    return first_half, ones_blk


def _kv_kernel(x_ref, g_ref, wk_ref, wv_ref, hn_ref, cos_ref, sin_ref, k_ref, vt_ref):
    x = x_ref[0]
    h = _rms(x, g_ref[...], EPS).astype(BF16)
    first_half, ones_blk = _rope_consts(SEQ_ROWS)
    cos = cos_ref[...]
    sin_signed = sin_ref[...]
    hn = hn_ref[...]
    for cc in range(D_MODEL // PROJ_COLS):
        cols = slice(cc * PROJ_COLS, (cc + 1) * PROJ_COLS)
        k = jnp.dot(h, wk_ref[:, cols], preferred_element_type=F32)
        v = jnp.dot(h, wv_ref[:, cols], preferred_element_type=F32)
        for i in range(PROJ_COLS // LANES):
            c = cc * (PROJ_COLS // LANES) + i
            sl = slice(i * LANES, (i + 1) * LANES)
            kr = _head_norm_rope(k[:, sl], hn, cos, sin_signed, first_half, ones_blk)
            k_ref[0, c] = kr.astype(BF16)
            vt_ref[0, c, 0, :LANES, :] = v[:, sl].astype(BF16).T
            pad_row = lax.broadcasted_iota(jnp.int32, (VT_ROWS - LANES, SEQ_ROWS), 0)
            vt_ref[0, c, 0, LANES:, :] = jnp.where(pad_row == 0, 1.0, 0.0).astype(BF16)


def _kv(x3d, g, wk, wv, hn, cos, sin_signed):
    b, s, _ = x3d.shape
    nt = s // SEQ_ROWS
    seq_spec = pl.BlockSpec((1, SEQ_ROWS, D_MODEL), lambda bi, si: (bi, si, 0))
    tab = pl.BlockSpec((SEQ_ROWS, LANES), lambda bi, si: (si, 0))
    return pl.pallas_call(
        _kv_kernel,
        out_shape=(jax.ShapeDtypeStruct((b, N_HEADS, s, LANES), BF16),
                   jax.ShapeDtypeStruct((b, N_HEADS, nt, VT_ROWS, SEQ_ROWS), BF16)),
        grid=(b, nt),
        in_specs=[seq_spec, _const_spec((1, D_MODEL)), _const_spec((D_MODEL, D_MODEL)),
                  _const_spec((D_MODEL, D_MODEL)), _const_spec((1, LANES)), tab, tab],
        out_specs=(pl.BlockSpec((1, N_HEADS, SEQ_ROWS, LANES), lambda bi, si: (bi, 0, si, 0)),
                   pl.BlockSpec((1, N_HEADS, 1, VT_ROWS, SEQ_ROWS),
                                lambda bi, si: (bi, 0, si, 0, 0))),
        compiler_params=_params(("arbitrary", "arbitrary")),
        name="kv_proj",
    )(x3d, g, wk, wv, hn, cos, sin_signed)


def _q_kernel(x_ref, g_ref, wqt_ref, hn_ref, cos_ref, sin_ref, qt_ref):
    x = x_ref[0]
    ht = _rms(x, g_ref[...], EPS).astype(BF16).T
    cos = cos_ref[...]
    sin_signed = sin_ref[...]
    hn = jnp.tile(hn_ref[...], (1, SEQ_ROWS // LANES))
    half = HEAD_DIM // 2

    def project(cc):
        return jnp.dot(wqt_ref[cc * PROJ_COLS:(cc + 1) * PROJ_COLS, :], ht,
                       preferred_element_type=F32)

    n_chunks = D_MODEL // PROJ_COLS
    for cc in range(n_chunks):
        q = project(cc)
        for i in range(PROJ_COLS // LANES):
            c = cc * (PROJ_COLS // LANES) + i
            parts = []
            for comp in range(LANES // HEAD_DIM):
                t = q[i * LANES + comp * HEAD_DIM:i * LANES + (comp + 1) * HEAD_DIM, :]
                ms = jnp.mean(t * t, axis=0, keepdims=True)
                rows = slice(comp * HEAD_DIM, (comp + 1) * HEAD_DIM)
                tn = t * lax.rsqrt(ms + EPS) * hn[rows, :]
                partner = jnp.concatenate([tn[half:], tn[:half]], axis=0)
                parts.append(tn * cos[rows, :] + partner * sin_signed[rows, :])
            qr = jnp.concatenate(parts, axis=0)
            qt_ref[0, c, 0] = (qr * (HEAD_DIM ** -0.5 * LOG2E)).astype(BF16)


def _qproj(x3d, layer, j, g, wqt, hn, cos_t, sin_t):
    b, s, _ = x3d.shape
    nt = s // SEQ_ROWS
    seq_spec = pl.BlockSpec((1, SEQ_ROWS, D_MODEL), lambda bi, si: (bi, si, 0))
    tab = pl.BlockSpec((LANES, SEQ_ROWS), lambda bi, si: (0, si))
    return pl.pallas_call(
        _q_kernel,
        out_shape=jax.ShapeDtypeStruct((b, N_HEADS, nt, LANES, SEQ_ROWS), BF16),
        grid=(b, nt),
        in_specs=[seq_spec, _layer_spec((1, D_MODEL), layer), _layer_spec((D_MODEL, D_MODEL), j),
                  _layer_spec((LANES, LANES), j), tab, tab],
        out_specs=pl.BlockSpec((1, N_HEADS, 1, LANES, SEQ_ROWS), lambda bi, si: (bi, 0, si, 0, 0)),
        compiler_params=_params(("arbitrary", "arbitrary")),
        name="q_proj",
    )(x3d, g, wqt, hn, cos_t, sin_t)


def _attn_kernel(bound_ref, qt_ref, k_ref, vt_ref, lq1_ref, lk1_ref, lq2_ref, lk2_ref, sn_ref,
                 o_ref, s_ref, p_ref, m_ref, acc_ref, *, lambda_init):
    qi = pl.program_id(2)
    bound = bound_ref[0]
    chains = []
    for hh in range(ATT_HEADS):
        qt = qt_ref[0, hh, 0]
        zero = jnp.zeros((HEAD_DIM, ATT_Q), BF16)
        chains += [(hh, jnp.concatenate([qt[:HEAD_DIM], zero], axis=0)),
                   (hh, jnp.concatenate([zero, qt[HEAD_DIM:]], axis=0))]
    n_chain = len(chains)
    n_lane_groups = ATT_Q // LANES

    half_k, half_q = ATT_K // 2, ATT_Q // 2

    def scores(c, j, skip_masked=False):
        hh, qz = chains[c]
        kj = k_ref[0, hh, pl.ds(pl.multiple_of(j * ATT_K, ATT_K), ATT_K), :]
        if skip_masked:
            s_ref[c, :half_k, :] = jnp.dot(kj[:half_k], qz, preferred_element_type=F32)
            s_ref[c, half_k:, half_q:] = jnp.dot(kj[half_k:], qz[:, half_q:],
                                                  preferred_element_type=F32)
        else:
            s_ref[c] = jnp.dot(kj, qz, preferred_element_type=F32)

    def exp_blocks(c, shift, diagonal):
        for rb in range(ATT_K // ATT_ROWS):
            rows = slice(rb * ATT_ROWS, (rb + 1) * ATT_ROWS)
            first_col = ((rb * ATT_ROWS) // CHUNK) * CHUNK if diagonal else 0
            for g in range(n_lane_groups):
                cols = slice(g * LANES, (g + 1) * LANES)
                if (g + 1) * LANES <= first_col:
                    p_ref[c, rows, cols] = jnp.zeros((ATT_ROWS, LANES), BF16)
                    continue
                if jnp.ndim(shift) == 0:
                    sb = shift
                else:
                    sb = jnp.tile(shift[:, cols], (ATT_ROWS // SUBLANES, 1))
                p = jnp.exp2(s_ref[c, rows, cols] - sb)
                if g * LANES < first_col:
                    lane = lax.broadcasted_iota(jnp.int32, (ATT_ROWS, LANES), 1)
                    p = jnp.where(lane >= first_col - g * LANES, p, 0.0)
                p_ref[c, rows, cols] = p.astype(BF16)

    def pv(c, j):
        return jnp.dot(vt_ref[0, chains[c][0], j], p_ref[c], preferred_element_type=F32)

    def bounded_tile(j, diagonal):
        scores(0, j, diagonal)
        scores(1, j, diagonal)
        for c in range(n_chain):
            exp_blocks(c, bound, diagonal)
            if c + 2 < n_chain:
                scores(c + 2, j, diagonal)
            if diagonal:
                vt = vt_ref[0, chains[c][0], j]
                acc_ref[c] = jnp.dot(vt[:, :half_k], p_ref[c, :half_k, :],
                                     preferred_element_type=F32)
                acc_ref[c, :, half_q:] += jnp.dot(vt[:, half_k:], p_ref[c, half_k:, half_q:],
                                                  preferred_element_type=F32)
            else:
                acc_ref[c] += pv(c, j)

    def online_tile(j, diagonal):
        scores(0, j)
        scores(1, j)
        for c in range(n_chain):
            if diagonal:
                k_chunk = lax.broadcasted_iota(jnp.int32, (ATT_K, ATT_Q), 0) // CHUNK
                q_chunk = lax.broadcasted_iota(jnp.int32, (ATT_K, ATT_Q), 1) // CHUNK
                s_all = jnp.where(k_chunk <= q_chunk, s_ref[c], -jnp.inf)
            else:
                s_all = s_ref[c]
            m_old = m_ref[c]
            m_new = jnp.maximum(m_old, jnp.max(s_all, axis=0, keepdims=True))
            alpha = jnp.exp2(m_old - m_new)
            m_ref[c] = m_new
            exp_blocks(c, m_new, diagonal)
            if c + 2 < n_chain:
                scores(c + 2, j)
            acc_ref[c] = jnp.tile(alpha, (VT_ROWS // SUBLANES, 1)) * acc_ref[c] + pv(c, j)

    def attend(tile, diagonal_first):
        def full_tile(j, carry):
            tile(j, False)
            return carry

        if diagonal_first:
            tile(qi, True)
            lax.fori_loop(0, qi, full_tile, 0)
        else:
            acc_ref[...] = jnp.zeros(acc_ref.shape, F32)
            lax.fori_loop(0, qi, full_tile, 0)
            tile(qi, True)

        lam = (jnp.exp(jnp.sum(lq1_ref[...] * lk1_ref[...], axis=-1, keepdims=True))
               - jnp.exp(jnp.sum(lq2_ref[...] * lk2_ref[...], axis=-1, keepdims=True))
               + lambda_init)
        gain = jnp.tile(sn_ref[...], (1, n_lane_groups))
        for hh in range(ATT_HEADS):
            c1, c2 = 2 * hh, 2 * hh + 1
            inv_l1 = 1.0 / acc_ref[c1, LANES:LANES + 1, :]
            inv_l2 = 1.0 / acc_ref[c2, LANES:LANES + 1, :]
            ot = acc_ref[c1, :LANES, :] * inv_l1 - lam * (acc_ref[c2, :LANES, :] * inv_l2)
            ms = jnp.mean(ot * ot, axis=0, keepdims=True)
            yt = ot * lax.rsqrt(ms + SUBLN_EPS) * gain * (1.0 - lambda_init)
            o_ref[0, :, hh * LANES:(hh + 1) * LANES] = yt.astype(BF16).T

    @pl.when(bound <= STABILIZER_LIMIT)
    def _():
        attend(bounded_tile, diagonal_first=True)

    @pl.when(jnp.logical_not(bound <= STABILIZER_LIMIT))
    def _():
        m_ref[...] = jnp.full(m_ref.shape, -jnp.inf, F32)
        attend(online_tile, diagonal_first=False)


def _attn(qt, k, vt, j, bound, lq1, lk1, lq2, lk2, sn, lambda_init):
    b, nh, s, _ = k.shape
    nt = s // ATT_K
    vec = _const_spec((1, HEAD_DIM))
    lvec = _layer_spec((1, HEAD_DIM), j)
    n_chain = 2 * ATT_HEADS
    return pl.pallas_call(
        functools.partial(_attn_kernel, lambda_init=lambda_init),
        out_shape=jax.ShapeDtypeStruct((b, s, D_MODEL), BF16),
        grid=(b, nh // ATT_HEADS, s // ATT_Q),
        in_specs=[pl.BlockSpec(memory_space=pltpu.SMEM),
                  pl.BlockSpec((1, ATT_HEADS, 1, LANES, ATT_Q),
                               lambda bi, hi, qi: (bi, hi, qi, 0, 0)),
                  pl.BlockSpec((1, ATT_HEADS, s, LANES), lambda bi, hi, qi: (bi, hi, 0, 0)),
                  pl.BlockSpec((1, ATT_HEADS, nt, VT_ROWS, ATT_K),
                               lambda bi, hi, qi: (bi, hi, 0, 0, 0)),
                  lvec, vec, lvec, vec, _layer_spec((LANES, LANES), j)],
        out_specs=pl.BlockSpec((1, ATT_Q, ATT_HEADS * LANES), lambda bi, hi, qi: (bi, qi, hi)),
        scratch_shapes=[pltpu.VMEM((n_chain, ATT_K, ATT_Q), F32),
                        pltpu.VMEM((n_chain, ATT_K, ATT_Q), BF16),
                        pltpu.VMEM((n_chain, SUBLANES, ATT_Q), F32),
                        pltpu.VMEM((n_chain, VT_ROWS, ATT_Q), F32)],
        compiler_params=_params(("arbitrary", "arbitrary", "arbitrary")),
        name="diff_attn",
    )(bound, qt, k, vt, lq1, lk1, lq2, lk2, sn)


def _rope_tables(s):
    pos = jnp.arange(s, dtype=F32)
    inv_freq = ROPE_THETA ** (-jnp.arange(0, HEAD_DIM, 2, dtype=F32) / HEAD_DIM)
    ang = pos[:, None] * inv_freq[None, :]
    ang = jnp.concatenate([ang, ang], axis=-1)
    cos, sin = jnp.cos(ang), jnp.sin(ang)
    half = HEAD_DIM // 2
    sin_signed = jnp.concatenate([-sin[:, :half], sin[:, half:]], axis=-1)
    reps = LANES // HEAD_DIM
    return jnp.tile(cos, (1, reps)), jnp.tile(sin_signed, (1, reps))


def kernel(x, ffn1_norm, ffn1_w_gate, ffn1_w_up, ffn1_w_down, ffn2_norm, ffn2_w_gate, ffn2_w_up, ffn2_w_down, mix_norm, rec_w_in, rec_conv_w, rec_conv_b, rec_w_a, rec_b_a, rec_w_x, rec_b_x, rec_lambda, rec_w_out, kv_norm, w_k, w_v, k_norm, lambda_k1, lambda_k2, attn_w_q, q_norm, lambda_q1, lambda_q2, sub_norm, attn_w_o):
    b, s, d = x.shape
    m = b * s
    reps = LANES // HEAD_DIM
    cos, sin_signed = _rope_tables(s)

    f1 = (ffn1_w_gate, ffn1_w_up, ffn1_w_down)
    f2 = (ffn2_w_gate, ffn2_w_up, ffn2_w_down)
    w_in = rec_w_in.astype(BF16)
    w_ax = jnp.concatenate([rec_w_a, rec_w_x], axis=-1).astype(BF16)
    w_out = rec_w_out.astype(BF16)
    wk, wv = w_k.astype(BF16), w_v.astype(BF16)
    wqt = attn_w_q.astype(BF16).transpose(0, 2, 1)
    wo = attn_w_o.astype(BF16)

    def row(vec):
        return vec.reshape(1, -1)

    def rows(stacked):
        return stacked.reshape(stacked.shape[0], 1, stacked.shape[1])

    g1, g2, gmix = rows(ffn1_norm), rows(ffn2_norm), rows(mix_norm)
    conv_b, b_a, b_x, lam = rows(rec_conv_b), rows(rec_b_a), rows(rec_b_x), rows(rec_lambda)
    q_hn = jnp.broadcast_to(jnp.tile(q_norm, (1, reps))[:, :, None], (q_norm.shape[0], LANES, LANES))
    cos_t, sin_t = cos.T, sin_signed.T
    lq1, lq2 = rows(lambda_q1), rows(lambda_q2)
    sub_g = jnp.broadcast_to(sub_norm[:, :, None], sub_norm.shape + (LANES,))

    k_shared = vt_shared = None
    x2 = x.reshape(m, d)
    for layer in range(DEPTH):
        if layer == N_A_LAYERS:
            k_shared, vt_shared = _kv(x2.reshape(b, s, d), row(kv_norm), wk, wv,
                                      row(jnp.tile(k_norm, reps)), cos, sin_signed)
        x2 = _ffn(x2, layer, g1, *f1)
        if layer < N_A_LAYERS:
            a = layer
            x2 = _rec(x2.reshape(b, s, d), layer, a, gmix, w_in, rec_conv_w, conv_b, w_ax, b_a,
                      b_x, lam, w_out).reshape(m, d)
            x2 = _ffn(x2, layer, g2, *f2)
        else:
            j = layer - N_A_LAYERS
            lambda_init = 0.8 - 0.6 * math.exp(-0.3 * layer)
            qt = _qproj(x2.reshape(b, s, d), layer, j, gmix, wqt, q_hn, cos_t, sin_t)
            bound = (SCORE_BOUND_MARGIN * HEAD_DIM ** 0.5 * LOG2E
                     * jnp.max(jnp.abs(q_norm[j])) * jnp.max(jnp.abs(k_norm))).reshape(1)
            o = _attn(qt, k_shared, vt_shared, j, bound, lq1, row(lambda_k1), lq2, row(lambda_k2),
                      sub_g, lambda_init)
            x2 = _ffn(x2, layer, g2, *f2, attn=(o.reshape(m, d), wo, j))
    return x2.reshape(b, s, d)
```
